```python
import jax, jax.numpy as jnp
from jax import lax
import numpy as np

D_MODEL = 2048
BATCH = 8
SEQ = 2048
DEPTH = 1
DEC_BATCH = 128
DEC_SEQ = 1
PAST_LEN = 2048
PAGE_SIZE = 128

SB_HEADS = 16
SB_HEAD_DIM = 64
SB_WIDTH = SB_HEADS * SB_HEAD_DIM
Q_BLOCK = 128
SB_BIAS_INIT = -6.0
SGU_GROUPS = 8
SGU_GROUP_DIM = 128
SGU_WIDTH = SGU_GROUPS * SGU_GROUP_DIM
CHUNK = 128
N_EXPERTS = 64
TOP_K = 6
D_EXPERT = 1408
D_SHARED = 1408
ROUTED_SCALE = 2.5
EXPERT_BLOCK = 128
LN_EPS = 1e-5
DN_ALPHA = (2.0 * DEPTH) ** 0.25
DN_BETA = (8.0 * DEPTH) ** -0.25
IN_SPLITS = (SB_WIDTH, 2 * SB_WIDTH, 3 * SB_WIDTH, 3 * SB_WIDTH + SGU_WIDTH,
             3 * SB_WIDTH + 2 * SGU_WIDTH, 3 * SB_WIDTH + 2 * SGU_WIDTH + D_MODEL)
D_IN = 3 * SB_WIDTH + 2 * SGU_WIDTH + 2 * D_MODEL

kernel_name = "stickbreak_sgu_gated_moe_decode_step"


def layer_norm(x, g, b):
    xf = x.astype(jnp.float32)
    mu = jnp.mean(xf, axis=-1, keepdims=True)
    var = jnp.mean(jnp.square(xf - mu), axis=-1, keepdims=True)
    y = (xf - mu) * lax.rsqrt(var + LN_EPS) * g.astype(jnp.float32) + b.astype(jnp.float32)
    return y.astype(x.dtype)


def in_projection(x, w_in):
    b, t = x.shape[:2]
    proj = x @ w_in
    q, k, v, u, sv, ga, gb = jnp.split(proj, IN_SPLITS, axis=-1)
    heads = lambda a: a.reshape(b, t, SB_HEADS, SB_HEAD_DIM)
    u = jax.nn.gelu(u, approximate=False)
    sv = jax.nn.gelu(sv, approximate=False)
    return heads(q), heads(k), heads(v), u, sv, ga, gb


def sb_weights(z, q_pos, k_pos):
    causal = k_pos[None, :] < q_pos[:, None]
    log_one_minus = jnp.where(causal, jax.nn.log_sigmoid(-z), 0.0)
    suffix = lax.cumsum(log_one_minus, axis=z.ndim - 1, reverse=True) - log_one_minus
    log_a = jax.nn.log_sigmoid(z) + suffix
    return jnp.where(causal, jnp.exp(log_a), 0.0)


def sb_prompt(q, k, v, sb_bias):
    b, s = q.shape[:2]
    nqb = s // Q_BLOCK
    pos = jnp.arange(s)
    scale = SB_HEAD_DIM ** -0.5
    bias = sb_bias.astype(jnp.float32)[None, :, None, None]
    q_blocks = q.reshape(b, nqb, Q_BLOCK, SB_HEADS, SB_HEAD_DIM).swapaxes(0, 1)
    pos_blocks = pos.reshape(nqb, Q_BLOCK)

    def block(args):
        q_blk, p_blk = args
        z = jnp.einsum('bqhd,bkhd->bhqk', q_blk, k).astype(jnp.float32) * scale + bias
        a = sb_weights(z, p_blk, pos).astype(v.dtype)
        return jnp.einsum('bhqk,bkhd->bqhd', a, v)

    o = lax.map(block, (q_blocks, pos_blocks))
    return o.swapaxes(0, 1).reshape(b, s, SB_WIDTH)


def sb_sample(q, k_new, v_new, k_past, v_past, sb_bias):
    b, t = q.shape[:2]
    p = k_past.shape[1]
    scale = SB_HEAD_DIM ** -0.5
    bias = sb_bias.astype(jnp.float32)[None, :, None, None]
    q_pos = p + jnp.arange(t)
    k_pos = jnp.arange(p + t)
    z = jnp.concatenate([jnp.einsum('bqhd,bkhd->bhqk', q, k_past),
                         jnp.einsum('bqhd,bkhd->bhqk', q, k_new)], axis=-1)
    a = sb_weights(z.astype(jnp.float32) * scale + bias, q_pos, k_pos).astype(v_new.dtype)
    o = (jnp.einsum('bhqk,bkhd->bqhd', a[..., :p], v_past)
         + jnp.einsum('bhqk,bkhd->bqhd', a[..., p:], v_new))
    return o.reshape(b, t, SB_WIDTH)


def sgu_mix(v, sgu_w, sgu_b):
    n = v.shape[-3]
    mask = jnp.tril(jnp.ones((n, n), dtype=sgu_w.dtype))
    w = sgu_w[:, :n, :n] * mask
    return jnp.einsum('gts,...sgc->...tgc', w, v) + sgu_b[:, :n].T[:, :, None]


def swiglu(x, w_gate, w_up, w_down):
    return (jax.nn.silu(x @ w_gate) * (x @ w_up)) @ w_down


def routed_experts(h, idx, gate, w_gate, w_up, w_down):
    t, d = h.shape
    n_assign = t * TOP_K
    expert = idx.reshape(n_assign)
    order = jnp.argsort(expert)
    e_sorted = expert[order]
    tok_sorted = order // TOP_K
    w_sorted = gate.reshape(n_assign)[order]
    counts = jnp.bincount(expert, length=N_EXPERTS)
    padded = (counts + EXPERT_BLOCK - 1) // EXPERT_BLOCK * EXPERT_BLOCK
    pad_end = jnp.cumsum(padded)
    pad_start = pad_end - padded
    raw_start = jnp.cumsum(counts) - counts
    dest = pad_start[e_sorted] + jnp.arange(n_assign) - raw_start[e_sorted]
    n_blocks = -(-n_assign // EXPERT_BLOCK) + N_EXPERTS
    buf = jnp.zeros((n_blocks * EXPERT_BLOCK, d), h.dtype).at[dest].set(h[tok_sorted])
    block_expert = jnp.minimum(
        jnp.searchsorted(pad_end, jnp.arange(n_blocks) * EXPERT_BLOCK, side='right'),
        N_EXPERTS - 1)

    def block(args):
        xb, e = args
        return swiglu(xb, w_gate[e], w_up[e], w_down[e])

    out = lax.map(block, (buf.reshape(n_blocks, EXPERT_BLOCK, d), block_expert))
    out = out.reshape(n_blocks * EXPERT_BLOCK, d)
    contrib = out[dest] * w_sorted[:, None].astype(out.dtype)
    return jnp.zeros((t, d), h.dtype).at[tok_sorted].add(contrib)


def moe_ffn(h, router_w, router_bias, w_gate, w_up, w_down, ws_gate, ws_up, ws_down):
    affinity = jax.nn.sigmoid((h @ router_w).astype(jnp.float32))
    _, idx = lax.top_k(affinity + router_bias.astype(jnp.float32), TOP_K)
    gate = jnp.take_along_axis(affinity, idx, axis=-1)
    gate = ROUTED_SCALE * gate / jnp.sum(gate, axis=-1, keepdims=True)
    routed = routed_experts(h, idx, gate, w_gate, w_up, w_down)
    return routed + swiglu(h, ws_gate, ws_up, ws_down)


def layer_tail(x, o_a, o_b, ga, gb, w_proj_a, w_proj_b, w_out, ln1_g, ln1_b,
               router_w, router_bias, exp_w_gate, exp_w_up, exp_w_down,
               sh_w_gate, sh_w_up, sh_w_down, ln2_g, ln2_b):
    merged = jax.nn.sigmoid(ga) * (o_a @ w_proj_a) + jax.nn.sigmoid(gb) * (o_b @ w_proj_b)
    h = layer_norm(DN_ALPHA * x + merged @ w_out, ln1_g, ln1_b)
    b, t, d = h.shape
    f = moe_ffn(h.reshape(b * t, d), router_w, router_bias, exp_w_gate, exp_w_up,
                exp_w_down, sh_w_gate, sh_w_up, sh_w_down).reshape(b, t, d)
    return layer_norm(DN_ALPHA * h + f, ln2_g, ln2_b)


def setup_inputs(seed: int = 0) -> dict:
    key = jax.random.key(seed)
    ks = jax.random.split(key, 26)
    f32 = jnp.float32
    nrm = lambda k, shape, scale: jax.random.normal(k, shape, f32) * scale
    n_pages = PAST_LEN // PAGE_SIZE
    n_used = DEC_BATCH * n_pages
    n_pool = n_used + (n_used + 3) // 4
    page_table = jax.random.permutation(ks[4], n_pool)[:n_used].reshape(
        DEC_BATCH, n_pages).astype(jnp.int32)
    L, D = DEPTH, D_MODEL
    return {
        "x_prompt": nrm(ks[0], (BATCH, SEQ, D), 1.0),
        "x_sample": nrm(ks[1], (DEC_BATCH, DEC_SEQ, D), 1.0),
        "cache_k": nrm(ks[2], (L, n_pool, PAGE_SIZE, SB_HEADS, SB_HEAD_DIM), 1.0),
        "cache_v": nrm(ks[3], (L, n_pool, PAGE_SIZE, SB_HEADS, SB_HEAD_DIM), 1.0),
        "page_table": page_table,
        "w_in": nrm(ks[5], (L, D, D_IN), D ** -0.5),
        "sb_bias": SB_BIAS_INIT + nrm(ks[24], (L, SB_HEADS), 0.5),
        "sgu_ln_g": 1.0 + nrm(ks[6], (L, SGU_WIDTH), 0.05),
        "sgu_ln_b": nrm(ks[7], (L, SGU_WIDTH), 0.02),
        "sgu_w": nrm(ks[8], (L, SGU_GROUPS, CHUNK, CHUNK), CHUNK ** -0.5),
        "sgu_b": 1.0 + nrm(ks[9], (L, SGU_GROUPS, CHUNK), 0.1),
        "w_proj_a": nrm(ks[10], (L, SB_WIDTH, D), SB_WIDTH ** -0.5),
        "w_proj_b": nrm(ks[11], (L, SGU_WIDTH, D), SGU_WIDTH ** -0.5),
        "w_out": nrm(ks[12], (L, D, D), DN_BETA * D ** -0.5),
        "ln1_g": 1.0 + nrm(ks[13], (L, D), 0.05),
        "ln1_b": nrm(ks[14], (L, D), 0.02),
        "router_w": nrm(ks[15], (L, D, N_EXPERTS), D ** -0.5),
        "router_bias": nrm(ks[16], (L, N_EXPERTS), 0.01),
        "exp_w_gate": nrm(ks[17], (L, N_EXPERTS, D, D_EXPERT), D ** -0.5),
        "exp_w_up": nrm(ks[18], (L, N_EXPERTS, D, D_EXPERT), D ** -0.5),
        "exp_w_down": nrm(ks[19], (L, N_EXPERTS, D_EXPERT, D), DN_BETA * D_EXPERT ** -0.5),
        "sh_w_gate": nrm(ks[20], (L, D, D_SHARED), D ** -0.5),
        "sh_w_up": nrm(ks[21], (L, D, D_SHARED), D ** -0.5),
        "sh_w_down": nrm(ks[22], (L, D_SHARED, D), DN_BETA * D_SHARED ** -0.5),
        "ln2_g": 1.0 + nrm(ks[23], (L, D), 0.05),
        "ln2_b": nrm(ks[25], (L, D), 0.02),
    }


def reference(x_prompt, x_sample, cache_k, cache_v, page_table, w_in, sb_bias, sgu_ln_g,
              sgu_ln_b, sgu_w, sgu_b, w_proj_a, w_proj_b, w_out, ln1_g, ln1_b, router_w,
              router_bias, exp_w_gate, exp_w_up, exp_w_down, sh_w_gate, sh_w_up, sh_w_down,
              ln2_g, ln2_b):
    n_pages = page_table.shape[1]
    past_len = n_pages * cache_k.shape[2]
    yp, ys = x_prompt, x_sample
    kp, vp, ksm, vsm, svs = [], [], [], [], []
    for l in range(DEPTH):
        tail_w = (w_proj_a[l], w_proj_b[l], w_out[l], ln1_g[l], ln1_b[l], router_w[l],
                  router_bias[l], exp_w_gate[l], exp_w_up[l], exp_w_down[l],
                  sh_w_gate[l], sh_w_up[l], sh_w_down[l], ln2_g[l], ln2_b[l])
        b, s = yp.shape[:2]
        q, k, v, u, sv, ga, gb = in_projection(yp, w_in[l])
        o_a = sb_prompt(q, k, v, sb_bias[l])
        svn = layer_norm(sv, sgu_ln_g[l], sgu_ln_b[l])
        mixed = sgu_mix(svn.reshape(b, s // CHUNK, CHUNK, SGU_GROUPS, SGU_GROUP_DIM),
                        sgu_w[l], sgu_b[l])
        o_b = u * mixed.reshape(b, s, SGU_WIDTH)
        yp = layer_tail(yp, o_a, o_b, ga, gb, *tail_w)
        kp.append(k)
        vp.append(v)
        bd, t = ys.shape[:2]
        q, k, v, u, sv, ga, gb = in_projection(ys, w_in[l])
        k_past = cache_k[l][page_table].reshape(bd, past_len, SB_HEADS, SB_HEAD_DIM)
        v_past = cache_v[l][page_table].reshape(bd, past_len, SB_HEADS, SB_HEAD_DIM)
        o_a = sb_sample(q, k, v, k_past, v_past, sb_bias[l])
        svn = layer_norm(sv, sgu_ln_g[l], sgu_ln_b[l])
        mixed = sgu_mix(svn.reshape(bd, t, SGU_GROUPS, SGU_GROUP_DIM), sgu_w[l], sgu_b[l])
        o_b = u * mixed.reshape(bd, t, SGU_WIDTH)
        ys = layer_tail(ys, o_a, o_b, ga, gb, *tail_w)
        ksm.append(k)
        vsm.append(v)
        svs.append(svn)
    return (yp, ys, jnp.stack(kp), jnp.stack(vp), jnp.stack(ksm), jnp.stack(vsm), jnp.stack(svs))
```

```python
import functools

import jax
import jax.numpy as jnp
from jax import lax
from jax.experimental import pallas as pl
from jax.experimental.pallas import tpu as pltpu

D_MODEL = 2048
SB_HEADS = 16
SB_HEAD_DIM = 64
SB_WIDTH = SB_HEADS * SB_HEAD_DIM
SGU_GROUPS = 8
SGU_GROUP_DIM = 128
SGU_WIDTH = SGU_GROUPS * SGU_GROUP_DIM
CHUNK = 128
N_EXPERTS = 64
TOP_K = 6
D_EXPERT = 1408
ROUTED_SCALE = 2.5
LN_EPS = 1e-5
SB_SCALE = SB_HEAD_DIM ** -0.5

V7X_LANES = 128
V7X_VMEM_BYTES = 64 * 2 ** 20

ROW_TILE = 512
COL_TILE = 512
ATT_TQ = 256
ATT_TK = 128
MOE_TILE = 512
MOE_KC = 512
MOE_NC = 512
TOK_TILE = 128

_BF = jnp.bfloat16
_F32 = jnp.float32


def _cparams(n_axes, vmem_mib):
    return pltpu.CompilerParams(
        dimension_semantics=("arbitrary",) * n_axes,
        vmem_limit_bytes=min(vmem_mib * 2 ** 20, V7X_VMEM_BYTES - 6 * 2 ** 20))


def _layer_norm(x, g, b):
    mu = jnp.mean(x, axis=-1, keepdims=True)
    xc = x - mu
    var = jnp.mean(xc * xc, axis=-1, keepdims=True)
    return xc * lax.rsqrt(var + LN_EPS) * g + b


def _gelu(x):
    return 0.5 * x * (1.0 + lax.erf(x * 0.7071067811865476))


def _sigmoid(x):
    return 1.0 / (1.0 + jnp.exp(-x))


def _softplus(z):
    return jnp.maximum(z, 0.0) + jnp.log(1.0 + jnp.exp(-jnp.abs(z)))


def _split_dot(x, u):
    hi = x.astype(_BF)
    lo = (x - hi.astype(_F32)).astype(_BF)
    return (jnp.dot(hi, u, preferred_element_type=_F32)
            + jnp.dot(lo, u, preferred_element_type=_F32))


def _in_sections(tn):
    widths = (SB_WIDTH, SB_WIDTH, SB_WIDTH, SGU_WIDTH, SGU_WIDTH, D_MODEL, D_MODEL)
    starts, s = [], 0
    for w in widths:
        starts.append(s // tn)
        s += w
    return starts, [w // tn for w in widths]


def _in_proj_kernel(x_ref, w_ref, q_ref, k_ref, kb_ref, v_ref, vb_ref, u_ref, sv_ref,
                    ga_ref, gb_ref, xb_ref, *, starts, counts):
    j = pl.program_id(1)

    @pl.when(j == 0)
    def _():
        xb_ref[...] = x_ref[...].astype(_BF)

    acc = jnp.dot(xb_ref[...], w_ref[...], preferred_element_type=_F32)

    def section(n):
        return (j >= starts[n]) & (j < starts[n] + counts[n])

    @pl.when(section(0))
    def _():
        q_ref[...] = (acc * SB_SCALE).astype(_BF)

    @pl.when(section(1))
    def _():
        k_ref[...] = acc
        kb_ref[...] = acc.astype(_BF)

    @pl.when(section(2))
    def _():
        v_ref[...] = acc
        vb_ref[...] = acc.astype(_BF)

    @pl.when(section(3))
    def _():
        u_ref[...] = _gelu(acc)

    @pl.when(section(4))
    def _():
        sv_ref[...] = _gelu(acc)

    @pl.when(section(5))
    def _():
        ga_ref[...] = _sigmoid(acc)

    @pl.when(section(6))
    def _():
        gb_ref[...] = _sigmoid(acc)


def _in_projection(x, w_bf, tm):
    m, d = x.shape
    tn = COL_TILE
    starts, counts = _in_sections(tn)
    n_tiles = w_bf.shape[1] // tn

    def out_spec(n):
        return pl.BlockSpec(
            (tm, tn), lambda i, j, n=n: (i, jnp.clip(j - starts[n], 0, counts[n] - 1)))

    shp = lambda w, dt: jax.ShapeDtypeStruct((m, w), dt)
    out_shape = (shp(SB_WIDTH, _BF), shp(SB_WIDTH, _F32), shp(SB_WIDTH, _BF),
                 shp(SB_WIDTH, _F32), shp(SB_WIDTH, _BF), shp(SGU_WIDTH, _F32),
                 shp(SGU_WIDTH, _F32), shp(D_MODEL, _F32), shp(D_MODEL, _F32))
    out_specs = (out_spec(0), out_spec(1), out_spec(1), out_spec(2), out_spec(2),
                 out_spec(3), out_spec(4), out_spec(5), out_spec(6))
    return pl.pallas_call(
        functools.partial(_in_proj_kernel, starts=starts, counts=counts),
        grid=(m // tm, n_tiles),
        in_specs=[pl.BlockSpec((tm, d), lambda i, j: (i, 0)),
                  pl.BlockSpec((d, tn), lambda i, j: (0, j))],
        out_specs=out_specs,
        out_shape=out_shape,
        scratch_shapes=[pltpu.VMEM((tm, d), _BF)],
        compiler_params=_cparams(2, 40),
        name="in_projection",
    )(x, w_bf)


def _sb_prompt_kernel(bias_ref, q_ref, k_ref, v_ref, o_ref, *, tq, tk):
    p = pl.program_id(1)
    i = pl.program_id(2)
    q2 = q_ref[...].astype(_F32)
    lane = lax.broadcasted_iota(jnp.int32, (tq, V7X_LANES), 1)
    row_pos = i * tq + lax.broadcasted_iota(jnp.int32, (tq, tk), 0)
    col = lax.broadcasted_iota(jnp.int32, (tq, tk), 1)
    ur = lax.broadcasted_iota(jnp.int32, (tk, tk), 0)
    uc = lax.broadcasted_iota(jnp.int32, (tk, tk), 1)
    upper = jnp.where(ur > uc, 1.0, 0.0).astype(_BF)
    n_kb = (i + 1) * (tq // tk)
    out = jnp.zeros((tq, V7X_LANES), _F32)
    for hh in range(2):
        in_head = (lane >= hh * SB_HEAD_DIM) & (lane < (hh + 1) * SB_HEAD_DIM)
        qm = jnp.where(in_head, q2, 0.0).astype(_BF)
        bias = bias_ref[2 * p + hh]

        def body(t, carry, qm=qm, bias=bias):
            csum, acc = carry
            off = pl.multiple_of((n_kb - 1 - t) * tk, tk)
            kblk = k_ref[pl.ds(off, tk), :]
            vblk = v_ref[pl.ds(off, tk), :]
            z = lax.dot_general(qm, kblk, (((1,), (1,)), ((), ())),
                                preferred_element_type=_F32) + bias
            causal = (off + col) < row_pos
            sp = _softplus(z)
            spm = jnp.where(causal, sp, 0.0)
            within = _split_dot(spm, upper)
            log_a = z - sp - within - csum
            a = jnp.where(causal, jnp.exp(log_a), 0.0).astype(_BF)
            acc = acc + jnp.dot(a, vblk, preferred_element_type=_F32)
            csum = csum + jnp.sum(spm, axis=1, keepdims=True)
            return csum, acc

        _, acc = lax.fori_loop(
            0, n_kb, body, (jnp.zeros((tq, 1), _F32), jnp.zeros((tq, V7X_LANES), _F32)))
        out = jnp.where(in_head, acc, out)
    o_ref[...] = out.astype(_BF)


def _sb_prompt(q_bf, k_bf, v_bf, sb_bias, b, s):
    tq, tk = min(ATT_TQ, s), min(ATT_TK, s)
    q3, k3, v3 = (a.reshape(b, s, SB_WIDTH) for a in (q_bf, k_bf, v_bf))
    kv_spec = pl.BlockSpec((None, s, V7X_LANES), lambda bi, p, i, bias: (bi, 0, p))
    q_spec = pl.BlockSpec((None, tq, V7X_LANES), lambda bi, p, i, bias: (bi, i, p))
    out = pl.pallas_call(
        functools.partial(_sb_prompt_kernel, tq=tq, tk=tk),
        grid_spec=pltpu.PrefetchScalarGridSpec(
            num_scalar_prefetch=1,
            grid=(b, SB_WIDTH // V7X_LANES, s // tq),
            in_specs=[q_spec, kv_spec, kv_spec],
            out_specs=q_spec),
        out_shape=jax.ShapeDtypeStruct((b, s, SB_WIDTH), _BF),
        compiler_params=_cparams(3, 32),
        name="sb_attention_prompt",
    )(sb_bias.astype(_F32), q3, k3, v3)
    return out.reshape(b * s, SB_WIDTH)


def _sb_sample_kernel(pt_ref, q_ref, kn_ref, vn_ref, bias_ref, kc_ref, vc_ref, o_ref,
                      acc_ref, csum_ref, *, page, past_len):
    j = pl.program_id(1)
    n_pages = pl.num_programs(1)
    head = lax.broadcasted_iota(jnp.int32, (SB_HEADS, SB_WIDTH), 0)
    lane = lax.broadcasted_iota(jnp.int32, (SB_HEADS, SB_WIDTH), 1)
    in_head = (lane >= head * SB_HEAD_DIM) & (lane < (head + 1) * SB_HEAD_DIM)
    q_bd = jnp.where(in_head, q_ref[...], 0.0).astype(_BF)
    bias = bias_ref[...]

    @pl.when(j == 0)
    def _():
        k_new = kn_ref[...].astype(_BF).astype(_F32)
        v_new = vn_ref[...].astype(_BF).astype(_F32)
        z_new = jnp.sum(q_bd.astype(_F32) * k_new, axis=1, keepdims=True) + bias
        k_pos = past_len + lax.broadcasted_iota(jnp.int32, (SB_HEADS, 1), 1)
        q_pos = past_len + lax.broadcasted_iota(jnp.int32, (SB_HEADS, 1), 1)
        causal = k_pos < q_pos
        sp_new = _softplus(z_new)
        a_new = jnp.where(causal, jnp.exp(z_new - sp_new), 0.0).astype(_BF).astype(_F32)
        acc_ref[...] = a_new * v_new
        csum_ref[...] = jnp.where(causal, sp_new, 0.0)

    ur = lax.broadcasted_iota(jnp.int32, (page, page), 0)
    uc = lax.broadcasted_iota(jnp.int32, (page, page), 1)
    upper = jnp.where(ur > uc, 1.0, 0.0).astype(_BF)
    kblk = kc_ref[...].astype(_BF)
    vblk = vc_ref[...].astype(_BF)
    z = lax.dot_general(q_bd, kblk, (((1,), (1,)), ((), ())),
                        preferred_element_type=_F32) + bias
    sp = _softplus(z)
    within = _split_dot(sp, upper)
    csum = csum_ref[...]
    a = jnp.exp(z - sp - within - csum).astype(_BF)
    acc_ref[...] += jnp.dot(a, vblk, preferred_element_type=_F32)
    csum_ref[...] = csum + jnp.sum(sp, axis=1, keepdims=True)

    @pl.when(j == n_pages - 1)
    def _():
        o_ref[...] = jnp.sum(jnp.where(in_head, acc_ref[...], 0.0), axis=0, keepdims=True)


def _sb_sample(q, k_new, v_new, cache_k, cache_v, page_table, sb_bias):
    bd = q.shape[0]
    n_pages = page_table.shape[1]
    n_pool, page = cache_k.shape[:2]
    ck = cache_k.reshape(n_pool, page, SB_WIDTH)
    cv = cache_v.reshape(n_pool, page, SB_WIDTH)
    row = pl.BlockSpec((None, 1, SB_WIDTH), lambda b, j, pt: (b, 0, 0))
    pg = pl.BlockSpec((None, page, SB_WIDTH),
                      lambda b, j, pt: (pt[b * n_pages + n_pages - 1 - j], 0, 0))
    r3 = lambda a: a.reshape(bd, 1, SB_WIDTH)
    out = pl.pallas_call(
        functools.partial(_sb_sample_kernel, page=page, past_len=n_pages * page),
        grid_spec=pltpu.PrefetchScalarGridSpec(
            num_scalar_prefetch=1,
            grid=(bd, n_pages),
            in_specs=[row, row, row,
                      pl.BlockSpec((SB_HEADS, 1), lambda b, j, pt: (0, 0)), pg, pg],
            out_specs=row,
            scratch_shapes=[pltpu.VMEM((SB_HEADS, SB_WIDTH), _F32),
                            pltpu.VMEM((SB_HEADS, 1), _F32)]),
        out_shape=jax.ShapeDtypeStruct((bd, 1, SB_WIDTH), _F32),
        compiler_params=_cparams(2, 16),
        name="sb_attention_sample",
    )(page_table.reshape(-1).astype(jnp.int32), r3(q), r3(k_new), r3(v_new),
      sb_bias.astype(_F32).reshape(SB_HEADS, 1), ck, cv)
    return out.reshape(bd, SB_WIDTH)


def _sgu_prompt_kernel(sv_ref, u_ref, w_ref, bias_ref, g_ref, b_ref, o_ref, *, n_chunks):
    r = lax.broadcasted_iota(jnp.int32, (CHUNK, CHUNK), 0)
    c = lax.broadcasted_iota(jnp.int32, (CHUNK, CHUNK), 1)
    lower = r >= c
    for ch in range(n_chunks):
        rows = slice(ch * CHUNK, (ch + 1) * CHUNK)
        svn = _layer_norm(sv_ref[rows, :], g_ref[...], b_ref[...]).astype(_BF)
        for g in range(SGU_GROUPS):
            cols = slice(g * SGU_GROUP_DIM, (g + 1) * SGU_GROUP_DIM)
            w = jnp.where(lower, w_ref[g], 0.0).astype(_BF)
            mixed = jnp.dot(w, svn[:, cols], preferred_element_type=_F32) + bias_ref[:, cols]
            o_ref[rows, cols] = (u_ref[rows, cols] * mixed).astype(_BF)


def _sgu_prompt(sv, u, sgu_w, sgu_b, ln_g, ln_b, tm):
    m = sv.shape[0]
    bias_full = jnp.repeat(sgu_b.T, SGU_GROUP_DIM, axis=1)
    blk = pl.BlockSpec((tm, SGU_WIDTH), lambda i: (i, 0))
    full = lambda shape: pl.BlockSpec(shape, lambda i: (0,) * len(shape))
    return pl.pallas_call(
        functools.partial(_sgu_prompt_kernel, n_chunks=tm // CHUNK),
        grid=(m // tm,),
        in_specs=[blk, blk, full((SGU_GROUPS, CHUNK, CHUNK)), full((CHUNK, SGU_WIDTH)),
                  full((1, SGU_WIDTH)), full((1, SGU_WIDTH))],
        out_specs=blk,
        out_shape=jax.ShapeDtypeStruct((m, SGU_WIDTH), _BF),
        compiler_params=_cparams(1, 32),
        name="sgu_prompt",
    )(sv, u, sgu_w, bias_full, ln_g.reshape(1, -1), ln_b.reshape(1, -1))


def _sgu_sample_kernel(sv_ref, u_ref, w0_ref, b0_ref, g_ref, b_ref, svn_ref, o_ref):
    svn = _layer_norm(sv_ref[...], g_ref[...], b_ref[...])
    svn_ref[...] = svn
    w0 = w0_ref[...].astype(_BF).astype(_F32)
    mixed = w0 * svn.astype(_BF).astype(_F32) + b0_ref[...]
    o_ref[...] = (u_ref[...] * mixed).astype(_BF)


def _sgu_sample(sv, u, sgu_w, sgu_b, ln_g, ln_b):
    m = sv.shape[0]
    w0 = jnp.repeat(sgu_w[:, 0, 0], SGU_GROUP_DIM).reshape(1, SGU_WIDTH)
    b0 = jnp.repeat(sgu_b[:, 0], SGU_GROUP_DIM).reshape(1, SGU_WIDTH)
    blk = pl.BlockSpec((m, SGU_WIDTH), lambda i: (0, 0))
    vec = pl.BlockSpec((1, SGU_WIDTH), lambda i: (0, 0))
    return pl.pallas_call(
        _sgu_sample_kernel,
        grid=(1,),
        in_specs=[blk, blk, vec, vec, vec, vec],
        out_specs=(blk, blk),
        out_shape=(jax.ShapeDtypeStruct((m, SGU_WIDTH), _F32),
                   jax.ShapeDtypeStruct((m, SGU_WIDTH), _BF)),
        compiler_params=_cparams(1, 16),
        name="sgu_sample",
    )(sv, u, w0, b0, ln_g.reshape(1, -1), ln_b.reshape(1, -1))


def _merge_kernel(oa_ref, ob_ref, wa_ref, wb_ref, sga_ref, sgb_ref, m_ref):
    pa = jnp.dot(oa_ref[...], wa_ref[...], preferred_element_type=_F32)
    pb = jnp.dot(ob_ref[...], wb_ref[...], preferred_element_type=_F32)
    m_ref[...] = (sga_ref[...] * pa + sgb_ref[...] * pb).astype(_BF)


def _merge(o_a, o_b, wa_bf, wb_bf, sga, sgb, tm):
    m = o_a.shape[0]
    tn = COL_TILE
    act = pl.BlockSpec((tm, SB_WIDTH), lambda i, j: (i, 0))
    wsp = pl.BlockSpec((SB_WIDTH, tn), lambda i, j: (0, j))
    gsp = pl.BlockSpec((tm, tn), lambda i, j: (i, j))
    return pl.pallas_call(
        _merge_kernel,
        grid=(m // tm, D_MODEL // tn),
        in_specs=[act, act, wsp, wsp, gsp, gsp],
        out_specs=gsp,
        out_shape=jax.ShapeDtypeStruct((m, D_MODEL), _BF),
        compiler_params=_cparams(2, 32),
        name="gated_merge",
    )(o_a, o_b, wa_bf, wb_bf, sga, sgb)


def _out_router_kernel(m_ref, x_ref, wo_ref, g_ref, b_ref, rw_ref, rb_ref,
                       h_ref, idx_ref, gate_ref, *, alpha):
    tm = m_ref.shape[0]
    y = alpha * x_ref[...] + jnp.dot(m_ref[...], wo_ref[...], preferred_element_type=_F32)
    h = _layer_norm(y, g_ref[...], b_ref[...])
    h_ref[...] = h
    logits = jnp.dot(h.astype(_BF), rw_ref[...], preferred_element_type=_F32)
    aff = _sigmoid(logits)
    sel = aff + rb_ref[...]
    e_iota = lax.broadcasted_iota(jnp.int32, (tm, N_EXPERTS), 1).astype(_F32)
    lane = lax.broadcasted_iota(jnp.int32, (tm, V7X_LANES), 1)
    idx_out = jnp.zeros((tm, V7X_LANES), _F32)
    gate_out = jnp.zeros((tm, V7X_LANES), _F32)
    gsum = jnp.zeros((tm, 1), _F32)
    for k in range(TOP_K):
        best = jnp.max(sel, axis=1, keepdims=True)
        ik = jnp.min(jnp.where(sel == best, e_iota, float(N_EXPERTS)), axis=1, keepdims=True)
        hit = e_iota == ik
        gk = jnp.sum(jnp.where(hit, aff, 0.0), axis=1, keepdims=True)
        sel = jnp.where(hit, -jnp.inf, sel)
        idx_out = jnp.where(lane == k, ik, idx_out)
        gate_out = jnp.where(lane == k, gk, gate_out)
        gsum = gsum + gk
    idx_ref[...] = idx_out.astype(jnp.int32)
    gate_ref[...] = ROUTED_SCALE * gate_out / gsum


def _out_router(merged, x, wo_bf, ln_g, ln_b, rw_bf, rbias, alpha, tm):
    m = x.shape[0]
    row = lambda w: pl.BlockSpec((tm, w), lambda i: (i, 0))
    full = lambda shape: pl.BlockSpec(shape, lambda i: (0, 0))
    return pl.pallas_call(
        functools.partial(_out_router_kernel, alpha=alpha),
        grid=(m // tm,),
        in_specs=[row(D_MODEL), row(D_MODEL), full((D_MODEL, D_MODEL)), full((1, D_MODEL)),
                  full((1, D_MODEL)), full((D_MODEL, N_EXPERTS)), full((1, N_EXPERTS))],
        out_specs=(row(D_MODEL), row(V7X_LANES), row(V7X_LANES)),
        out_shape=(jax.ShapeDtypeStruct((m, D_MODEL), _F32),
                   jax.ShapeDtypeStruct((m, V7X_LANES), jnp.int32),
                   jax.ShapeDtypeStruct((m, V7X_LANES), _F32)),
        compiler_params=_cparams(1, 48),
        name="out_proj_router",
    )(merged, x, wo_bf, ln_g.reshape(1, -1), ln_b.reshape(1, -1), rw_bf,
      rbias.astype(_F32).reshape(1, -1))


def _rank_kernel(idx_ref, rank_ref, cnt_ref, carry_ref):
    t = pl.program_id(0)
    tt = idx_ref.shape[0]

    @pl.when(t == 0)
    def _():
        carry_ref[...] = jnp.zeros_like(carry_ref)

    idx = idx_ref[...]
    e_iota = lax.broadcasted_iota(jnp.int32, (tt, V7X_LANES), 1)
    hits = [e_iota == idx[:, k:k + 1] for k in range(TOP_K)]
    member = jnp.zeros((tt, V7X_LANES), _F32)
    for hit in hits:
        member = member + jnp.where(hit, 1.0, 0.0)
    r = lax.broadcasted_iota(jnp.int32, (tt, tt), 0)
    c = lax.broadcasted_iota(jnp.int32, (tt, tt), 1)
    earlier = jnp.where(c < r, 1.0, 0.0).astype(_BF)
    before = jnp.dot(earlier, member.astype(_BF), preferred_element_type=_F32) + carry_ref[0:1, :]
    rank = jnp.zeros((tt, V7X_LANES), _F32)
    for k, hit in enumerate(hits):
        rk = jnp.sum(jnp.where(hit, before, 0.0), axis=1, keepdims=True)
        rank = jnp.where(e_iota == k, rk, rank)
    rank_ref[...] = rank.astype(jnp.int32)
    carry_ref[0:1, :] = carry_ref[0:1, :] + jnp.sum(member, axis=0, keepdims=True)
    cnt_ref[...] = carry_ref[...]


def _ranks(idx_lanes):
    t = idx_lanes.shape[0]
    tt = TOK_TILE
    blk = pl.BlockSpec((tt, V7X_LANES), lambda i: (i, 0))
    return pl.pallas_call(
        _rank_kernel,
        grid=(t // tt,),
        in_specs=[blk],
        out_specs=(blk, pl.BlockSpec((8, V7X_LANES), lambda i: (0, 0))),
        out_shape=(jax.ShapeDtypeStruct((t, V7X_LANES), jnp.int32),
                   jax.ShapeDtypeStruct((8, V7X_LANES), _F32)),
        scratch_shapes=[pltpu.VMEM((8, V7X_LANES), _F32)],
        compiler_params=_cparams(1, 16),
        name="expert_ranks",
    )(idx_lanes)


def _row_copy(h_hbm, xbuf, sem, tok, r):
    return pltpu.make_async_copy(h_hbm.at[pl.ds(tok, 1)], xbuf.at[pl.ds(r, 1)], sem)


def _moe_kernel(te_ref, nu_ref, tokc_ref, tokn_ref, h_hbm, wg_ref, wu_ref, wd_ref, o_ref,
                xbuf, accg, accu, act, sem, *, tile, n_kc):
    i = pl.program_id(0)
    s = pl.program_id(1)
    n_used = nu_ref[0]
    live = i < n_used

    def gather(tok_ref):
        def body(r, carry):
            _row_copy(h_hbm, xbuf, sem, tok_ref[0, 0, r], r).start()
            return carry
        lax.fori_loop(0, tile, body, 0)

    @pl.when((i == 0) & (s == 0))
    def _():
        gather(tokc_ref)

    @pl.when(live & (s == 0))
    def _():
        def body(r, carry):
            _row_copy(h_hbm, xbuf, sem, 0, r).wait()
            return carry
        lax.fori_loop(0, tile, body, 0)

    kc = xbuf.shape[1] // n_kc
    for c in range(n_kc):
        @pl.when(live & (s == c))
        def _(c=c):
            xk = xbuf[:, c * kc:(c + 1) * kc].astype(_BF)
            g = jnp.dot(xk, wg_ref[...].astype(_BF), preferred_element_type=_F32)
            u = jnp.dot(xk, wu_ref[...].astype(_BF), preferred_element_type=_F32)
            if c > 0:
                g = g + accg[...]
                u = u + accu[...]
            if c < n_kc - 1:
                accg[...] = g
                accu[...] = u
            else:
                act[...] = (g * _sigmoid(g) * u).astype(_BF)

    @pl.when(live & (s == n_kc) & (i + 1 < n_used))
    def _():
        gather(tokn_ref)

    @pl.when(live & (s >= n_kc))
    def _():
        o_ref[...] = jnp.dot(act[...], wd_ref[...].astype(_BF), preferred_element_type=_F32)

    @pl.when(jnp.logical_not(live) & (s >= n_kc))
    def _():
        o_ref[...] = jnp.zeros_like(o_ref)


def _routed_experts(h_all, tok_of_row, tile_expert, n_used, w_gate, w_up, w_down):
    tile = MOE_TILE
    n_tiles = tile_expert.shape[0]
    d, f = w_gate.shape[1:]
    n_kc, n_nc = d // MOE_KC, d // MOE_NC
    tok3 = tok_of_row.reshape(n_tiles, 1, tile)

    def k_idx(i, s, te, nu):
        return jnp.where(i < nu[0], jnp.minimum(s, n_kc - 1), n_kc - 1)

    def n_idx(s):
        return jnp.clip(s - n_kc, 0, n_nc - 1)

    smem_row = lambda fn: pl.BlockSpec((1, 1, tile), fn, memory_space=pltpu.SMEM)
    in_specs = [
        smem_row(lambda i, s, te, nu: (i, 0, 0)),
        smem_row(lambda i, s, te, nu: (jnp.minimum(i + 1, n_tiles - 1), 0, 0)),
        pl.BlockSpec(memory_space=pl.ANY),
        pl.BlockSpec((None, MOE_KC, f), lambda i, s, te, nu: (te[i], k_idx(i, s, te, nu), 0)),
        pl.BlockSpec((None, MOE_KC, f), lambda i, s, te, nu: (te[i], k_idx(i, s, te, nu), 0)),
        pl.BlockSpec((None, f, MOE_NC), lambda i, s, te, nu: (te[i], 0, n_idx(s))),
    ]
    return pl.pallas_call(
        functools.partial(_moe_kernel, tile=tile, n_kc=n_kc),
        grid_spec=pltpu.PrefetchScalarGridSpec(
            num_scalar_prefetch=2,
            grid=(n_tiles, n_kc + n_nc),
            in_specs=in_specs,
            out_specs=pl.BlockSpec((tile, MOE_NC), lambda i, s, te, nu: (i, n_idx(s))),
            scratch_shapes=[pltpu.VMEM((tile, d), _F32), pltpu.VMEM((tile, f), _F32),
                            pltpu.VMEM((tile, f), _F32), pltpu.VMEM((tile, f), _BF),
                            pltpu.SemaphoreType.DMA(())]),
        out_shape=jax.ShapeDtypeStruct((n_tiles * tile, d), _F32),
        compiler_params=_cparams(2, 52),
        name="routed_experts",
    )(tile_expert, n_used, tok3, tok3, h_all, w_gate, w_up, w_down)


def _shared_kernel(x_ref, wg_ref, wu_ref, wd_ref, o_ref, accg, accu, act, *, n_kc):
    s = pl.program_id(1)
    for c in range(n_kc):
        @pl.when(s == c)
        def _(c=c):
            xk = x_ref[...].astype(_BF)
            g = jnp.dot(xk, wg_ref[...], preferred_element_type=_F32)
            u = jnp.dot(xk, wu_ref[...], preferred_element_type=_F32)
            if c > 0:
                g = g + accg[...]
                u = u + accu[...]
            if c < n_kc - 1:
                accg[...] = g
                accu[...] = u
            else:
                act[...] = (g * _sigmoid(g) * u).astype(_BF)

    @pl.when(s >= n_kc)
    def _():
        o_ref[...] = jnp.dot(act[...], wd_ref[...], preferred_element_type=_F32)


def _shared_expert(h, wg_bf, wu_bf, wd_bf, tm):
    m, d = h.shape
    f = wg_bf.shape[1]
    n_kc, n_nc = d // MOE_KC, d // MOE_NC
    k_idx = lambda s: jnp.minimum(s, n_kc - 1)
    n_idx = lambda s: jnp.clip(s - n_kc, 0, n_nc - 1)
    return pl.pallas_call(
        functools.partial(_shared_kernel, n_kc=n_kc),
        grid=(m // tm, n_kc + n_nc),
        in_specs=[pl.BlockSpec((tm, MOE_KC), lambda i, s: (i, k_idx(s))),
                  pl.BlockSpec((MOE_KC, f), lambda i, s: (k_idx(s), 0)),
                  pl.BlockSpec((MOE_KC, f), lambda i, s: (k_idx(s), 0)),
                  pl.BlockSpec((f, MOE_NC), lambda i, s: (0, n_idx(s)))],
        out_specs=pl.BlockSpec((tm, MOE_NC), lambda i, s: (i, n_idx(s))),
        out_shape=jax.ShapeDtypeStruct((m, d), _F32),
        scratch_shapes=[pltpu.VMEM((tm, f), _F32), pltpu.VMEM((tm, f), _F32),
                        pltpu.VMEM((tm, f), _BF)],
        compiler_params=_cparams(2, 40),
        name="shared_expert",
    )(h, wg_bf, wu_bf, wd_bf)


def _combine_kernel(dest_ref, gate_ref, h_ref, sh_ref, outs_hbm, g_ref, b_ref, yp_ref, ys_ref,
                    gbuf, sem, *, alpha, n_prompt_tiles):
    i = pl.program_id(0)
    tt = h_ref.shape[0]

    def copy(dst_row, k, r):
        return pltpu.make_async_copy(outs_hbm.at[pl.ds(dst_row, 1)],
                                     gbuf.at[k, pl.ds(r, 1)], sem)

    def issue(r, carry):
        for k in range(TOP_K):
            copy(dest_ref[0, 0, r * TOP_K + k], k, r).start()
        return carry

    def drain(r, carry):
        for k in range(TOP_K):
            copy(0, k, r).wait()
        return carry

    lax.fori_loop(0, tt, issue, 0)
    acc = alpha * h_ref[...] + sh_ref[...]
    gate = gate_ref[...]
    lax.fori_loop(0, tt, drain, 0)
    for k in range(TOP_K):
        acc = acc + gate[:, k:k + 1] * gbuf[k]
    y = _layer_norm(acc, g_ref[...], b_ref[...])

    @pl.when(i < n_prompt_tiles)
    def _():
        yp_ref[...] = y

    @pl.when(i >= n_prompt_tiles)
    def _():
        ys_ref[...] = y


def _combine(dest, gate_lanes, h_all, shared_all, outs, ln_g, ln_b, alpha, n_prompt, n_sample):
    tt = TOK_TILE
    t_all, d = h_all.shape
    n_pt, n_st = n_prompt // tt, n_sample // tt
    dest3 = dest.reshape(t_all // tt, 1, tt * TOP_K)
    row = lambda w: pl.BlockSpec((tt, w), lambda i: (i, 0))
    vec = pl.BlockSpec((1, d), lambda i: (0, 0))
    return pl.pallas_call(
        functools.partial(_combine_kernel, alpha=alpha, n_prompt_tiles=n_pt),
        grid=(n_pt + n_st,),
        in_specs=[pl.BlockSpec((1, 1, tt * TOP_K), lambda i: (i, 0, 0), memory_space=pltpu.SMEM),
                  row(V7X_LANES), row(d), row(d), pl.BlockSpec(memory_space=pl.ANY), vec, vec],
        out_specs=(pl.BlockSpec((tt, d), lambda i: (jnp.minimum(i, n_pt - 1), 0)),
                   pl.BlockSpec((tt, d), lambda i: (jnp.clip(i - n_pt, 0, n_st - 1), 0))),
        out_shape=(jax.ShapeDtypeStruct((n_prompt, d), _F32),
                   jax.ShapeDtypeStruct((n_sample, d), _F32)),
        scratch_shapes=[pltpu.VMEM((TOP_K, tt, d), _F32), pltpu.SemaphoreType.DMA(())],
        compiler_params=_cparams(1, 32),
        name="moe_combine",
    )(dest3, gate_lanes, h_all, shared_all, outs, ln_g.reshape(1, -1), ln_b.reshape(1, -1))


def _routing_tables(idx_lanes, rank_lanes, counts_row):
    n_tok = idx_lanes.shape[0]
    n_assign = n_tok * TOP_K
    n_tiles = -(-n_assign // MOE_TILE) + N_EXPERTS
    idx = idx_lanes[:, :TOP_K]
    counts = counts_row[0, :N_EXPERTS].astype(jnp.int32)
    padded = (counts + MOE_TILE - 1) // MOE_TILE * MOE_TILE
    pad_end = jnp.cumsum(padded)
    pad_start = pad_end - padded
    dest = pad_start[idx] + rank_lanes[:, :TOP_K]
    n_used = pad_end[-1] // MOE_TILE
    tiles = jnp.minimum(jnp.arange(n_tiles, dtype=jnp.int32), n_used - 1)
    tile_expert = jnp.minimum(
        jnp.searchsorted(pad_end, tiles * MOE_TILE, side='right'), N_EXPERTS - 1)
    tok_of_row = jnp.zeros((n_tiles * MOE_TILE,), jnp.int32).at[dest.reshape(-1)].set(
        jnp.arange(n_assign, dtype=jnp.int32) // TOP_K)
    return (dest.astype(jnp.int32), tok_of_row, tile_expert.astype(jnp.int32),
            n_used.astype(jnp.int32).reshape(1))


def _mixers_to_router(x2, o_a, o_b, sga, sgb, wts, alpha, tm):
    merged = _merge(o_a, o_b, wts["w_proj_a"], wts["w_proj_b"], sga, sgb, tm)
    return _out_router(merged, x2, wts["w_out"], wts["ln1_g"], wts["ln1_b"],
                       wts["router_w"], wts["router_bias"], alpha, tm)


def _layer(yp, ys, cache_k, cache_v, page_table, wts, alpha):
    b, s, d = yp.shape
    bd, t, _ = ys.shape
    n_p, n_s = b * s, bd * t
    xp, xs = yp.reshape(n_p, d), ys.reshape(n_s, d)

    q, k, kb, v, vb, u, sv, sga, sgb = _in_projection(xp, wts["w_in"], ROW_TILE)
    o_a = _sb_prompt(q, kb, vb, wts["sb_bias"], b, s)
    o_b = _sgu_prompt(sv, u, wts["sgu_w"], wts["sgu_b"], wts["sgu_ln_g"], wts["sgu_ln_b"], ROW_TILE)
    h_p, idx_p, gate_p = _mixers_to_router(xp, o_a, o_b, sga, sgb, wts, alpha, ROW_TILE)
    k_p = k.reshape(b, s, SB_HEADS, SB_HEAD_DIM)
    v_p = v.reshape(b, s, SB_HEADS, SB_HEAD_DIM)

    q, k, kb, v, vb, u, sv, sga, sgb = _in_projection(xs, wts["w_in"], n_s)
    o_a = _sb_sample(q.astype(_F32), k, v, cache_k, cache_v, page_table, wts["sb_bias"])
    svn, o_b = _sgu_sample(sv, u, wts["sgu_w"], wts["sgu_b"], wts["sgu_ln_g"], wts["sgu_ln_b"])
    h_s, idx_s, gate_s = _mixers_to_router(xs, o_a.astype(_BF), o_b, sga, sgb, wts, alpha, n_s)
    k_s = k.reshape(bd, t, SB_HEADS, SB_HEAD_DIM)
    v_s = v.reshape(bd, t, SB_HEADS, SB_HEAD_DIM)

    h_all = jnp.concatenate([h_p, h_s], axis=0)
    idx_all = jnp.concatenate([idx_p, idx_s], axis=0)
    gate_all = jnp.concatenate([gate_p, gate_s], axis=0)
    rank, counts = _ranks(idx_all)
    dest, tok_of_row, tile_expert, n_used = _routing_tables(idx_all, rank, counts)
    outs = _routed_experts(h_all, tok_of_row, tile_expert, n_used,
                           wts["exp_w_gate"], wts["exp_w_up"], wts["exp_w_down"])
    sh_p = _shared_expert(h_p, wts["sh_w_gate"], wts["sh_w_up"], wts["sh_w_down"], ROW_TILE)
    sh_s = _shared_expert(h_s, wts["sh_w_gate"], wts["sh_w_up"], wts["sh_w_down"], n_s)
    shared_all = jnp.concatenate([sh_p, sh_s], axis=0)
    yp2, ys2 = _combine(dest, gate_all, h_all, shared_all, outs, wts["ln2_g"], wts["ln2_b"],
                        alpha, n_p, n_s)
    return (yp2.reshape(b, s, d), ys2.reshape(bd, t, d), k_p, v_p, k_s, v_s,
            svn.reshape(bd, t, SGU_WIDTH))


def kernel(x_prompt, x_sample, cache_k, cache_v, page_table, w_in, sb_bias, sgu_ln_g, sgu_ln_b, sgu_w, sgu_b, w_proj_a, w_proj_b, w_out, ln1_g, ln1_b, router_w, router_bias, exp_w_gate, exp_w_up, exp_w_down, sh_w_gate, sh_w_up, sh_w_down, ln2_g, ln2_b):
    depth = w_in.shape[0]
    alpha = (2.0 * depth) ** 0.25
    yp, ys = x_prompt, x_sample
    kp, vp, ksm, vsm, svs = [], [], [], [], []
    for l in range(depth):
        wts = {
            "w_in": w_in[l].astype(_BF), "sb_bias": sb_bias[l],
            "sgu_ln_g": sgu_ln_g[l], "sgu_ln_b": sgu_ln_b[l], "sgu_w": sgu_w[l], "sgu_b": sgu_b[l],
            "w_proj_a": w_proj_a[l].astype(_BF), "w_proj_b": w_proj_b[l].astype(_BF),
            "w_out": w_out[l].astype(_BF), "ln1_g": ln1_g[l], "ln1_b": ln1_b[l],
            "router_w": router_w[l].astype(_BF), "router_bias": router_bias[l],
            "exp_w_gate": exp_w_gate[l], "exp_w_up": exp_w_up[l], "exp_w_down": exp_w_down[l],
            "sh_w_gate": sh_w_gate[l].astype(_BF), "sh_w_up": sh_w_up[l].astype(_BF),
            "sh_w_down": sh_w_down[l].astype(_BF), "ln2_g": ln2_g[l], "ln2_b": ln2_b[l],
        }
        yp, ys, k_p, v_p, k_s, v_s, svn = _layer(yp, ys, cache_k[l], cache_v[l], page_table,
                                                 wts, alpha)
        kp.append(k_p)
        vp.append(v_p)
        ksm.append(k_s)
        vsm.append(v_s)
        svs.append(svn)
    return (yp, ys, jnp.stack(kp), jnp.stack(vp), jnp.stack(ksm), jnp.stack(vsm), jnp.stack(svs))
```

```python
import functools

import jax
import jax.numpy as jnp
from jax import lax
from jax.experimental import pallas as pl
from jax.experimental.pallas import tpu as pltpu

D_MODEL = 2048
SB_HEADS = 16
SB_HEAD_DIM = 64
SB_WIDTH = SB_HEADS * SB_HEAD_DIM
SGU_GROUPS = 8
SGU_GROUP_DIM = 128
SGU_WIDTH = SGU_GROUPS * SGU_GROUP_DIM
CHUNK = 128
N_EXPERTS = 64
TOP_K = 6
D_EXPERT = 1408
ROUTED_SCALE = 2.5
LN_EPS = 1e-5
SB_SCALE = SB_HEAD_DIM ** -0.5

V7X_LANES = 128
V7X_VMEM_BYTES = 64 * 2 ** 20

ROW_TILE = 512
COL_TILE = 512
ATT_TQ = 256
ATT_KG = 512
ATT_PAGES = 4
MOE_TILE = 512
MOE_KC = 512
MOE_NC = 1024
SHARED_NC = 512
TOK_TILE = 128
ROUTER_TILE = 256

_BF = jnp.bfloat16
_F32 = jnp.float32


def _cparams(n_axes, vmem_mib):
    return pltpu.CompilerParams(
        dimension_semantics=("arbitrary",) * n_axes,
        vmem_limit_bytes=min(vmem_mib * 2 ** 20, V7X_VMEM_BYTES - 6 * 2 ** 20))


def _layer_norm(x, g, b):
    mu = jnp.mean(x, axis=-1, keepdims=True)
    xc = x - mu
    var = jnp.mean(xc * xc, axis=-1, keepdims=True)
    return xc * lax.rsqrt(var + LN_EPS) * g + b


def _gelu(x):
    return 0.5 * x * (1.0 + lax.erf(x * 0.7071067811865476))


def _sigmoid(x):
    return 1.0 / (1.0 + jnp.exp(-x))


def _softplus(z):
    return jnp.maximum(z, 0.0) + jnp.log(1.0 + jnp.exp(-jnp.abs(z)))


def _split_dot(x, u):
    hi = x.astype(_BF)
    lo = (x - hi.astype(_F32)).astype(_BF)
    return (jnp.dot(hi, u, preferred_element_type=_F32)
            + jnp.dot(lo, u, preferred_element_type=_F32))


def _in_sections(tn):
    widths = (SB_WIDTH, SB_WIDTH, SB_WIDTH, SGU_WIDTH, SGU_WIDTH, D_MODEL, D_MODEL)
    starts, s = [], 0
    for w in widths:
        starts.append(s // tn)
        s += w
    return starts, [w // tn for w in widths]


def _in_proj_kernel(x_ref, w_ref, q_ref, k_ref, kb_ref, v_ref, vb_ref, u_ref, sv_ref,
                    ga_ref, gb_ref, xb_ref, *, starts, counts):
    j = pl.program_id(1)

    @pl.when(j == 0)
    def _():
        xb_ref[...] = x_ref[...].astype(_BF)

    acc = jnp.dot(xb_ref[...], w_ref[...], preferred_element_type=_F32)

    def section(n):
        return (j >= starts[n]) & (j < starts[n] + counts[n])

    @pl.when(section(0))
    def _():
        q_ref[...] = (acc * SB_SCALE).astype(_BF)

    @pl.when(section(1))
    def _():
        k_ref[...] = acc
        kb_ref[...] = acc.astype(_BF)

    @pl.when(section(2))
    def _():
        v_ref[...] = acc
        vb_ref[...] = acc.astype(_BF)

    @pl.when(section(3))
    def _():
        u_ref[...] = _gelu(acc)

    @pl.when(section(4))
    def _():
        sv_ref[...] = _gelu(acc)

    @pl.when(section(5))
    def _():
        ga_ref[...] = _sigmoid(acc)

    @pl.when(section(6))
    def _():
        gb_ref[...] = _sigmoid(acc)


def _in_projection(x, w_bf, tm):
    m, d = x.shape
    tn = COL_TILE
    starts, counts = _in_sections(tn)
    n_tiles = w_bf.shape[1] // tn

    def out_spec(n):
        return pl.BlockSpec(
            (tm, tn), lambda i, j, n=n: (i, jnp.clip(j - starts[n], 0, counts[n] - 1)))

    shp = lambda w, dt: jax.ShapeDtypeStruct((m, w), dt)
    out_shape = (shp(SB_WIDTH, _BF), shp(SB_WIDTH, _F32), shp(SB_WIDTH, _BF),
                 shp(SB_WIDTH, _F32), shp(SB_WIDTH, _BF), shp(SGU_WIDTH, _F32),
                 shp(SGU_WIDTH, _F32), shp(D_MODEL, _F32), shp(D_MODEL, _F32))
    out_specs = (out_spec(0), out_spec(1), out_spec(1), out_spec(2), out_spec(2),
                 out_spec(3), out_spec(4), out_spec(5), out_spec(6))
    return pl.pallas_call(
        functools.partial(_in_proj_kernel, starts=starts, counts=counts),
        grid=(m // tm, n_tiles),
        in_specs=[pl.BlockSpec((tm, d), lambda i, j: (i, 0)),
                  pl.BlockSpec((d, tn), lambda i, j: (0, j))],
        out_specs=out_specs,
        out_shape=out_shape,
        scratch_shapes=[pltpu.VMEM((tm, d), _BF)],
        compiler_params=_cparams(2, 40),
        name="in_projection",
    )(x, w_bf)


def _sb_prompt_kernel(bias_ref, q_ref, k_ref, v_ref, o_ref, *, tq, kg):
    p = pl.program_id(1)
    i = pl.program_id(2)
    nb = kg // V7X_LANES
    q2 = q_ref[...].astype(_F32)
    lane = lax.broadcasted_iota(jnp.int32, (tq, V7X_LANES), 1)
    row_pos = i * tq + lax.broadcasted_iota(jnp.int32, (tq, kg), 0)
    col = lax.broadcasted_iota(jnp.int32, (tq, kg), 1)
    ur = lax.broadcasted_iota(jnp.int32, (2 * V7X_LANES, 2 * V7X_LANES), 0)
    uc = lax.broadcasted_iota(jnp.int32, (2 * V7X_LANES, 2 * V7X_LANES), 1)
    later_in_block = jnp.where(((ur >= V7X_LANES) == (uc >= V7X_LANES)) & (ur > uc),
                               1.0, 0.0).astype(_BF)
    n_groups = ((i + 1) * tq + kg - 1) // kg
    in_head = [(lane >= hh * SB_HEAD_DIM) & (lane < (hh + 1) * SB_HEAD_DIM) for hh in range(2)]
    qm = [jnp.where(in_head[hh], q2, 0.0).astype(_BF) for hh in range(2)]
    bias = [bias_ref[2 * p + hh] for hh in range(2)]

    def body(t, carry):
        off = pl.multiple_of((n_groups - 1 - t) * kg, kg)
        kblk = k_ref[pl.ds(off, kg), :]
        vblk = v_ref[pl.ds(off, kg), :]
        causal = (off + col) < row_pos
        new = []
        for hh in range(2):
            run, acc = carry[2 * hh], carry[2 * hh + 1]
            z = lax.dot_general(qm[hh], kblk, (((1,), (1,)), ((), ())),
                                preferred_element_type=_F32) + bias[hh]
            sp = _softplus(z)
            spm = jnp.where(causal, sp, 0.0)
            spb = spm.astype(_BF)
            pairs = jnp.concatenate(
                [spb[:, g * 2 * V7X_LANES:(g + 1) * 2 * V7X_LANES] for g in range(nb // 2)], axis=0)
            within = jnp.dot(pairs, later_in_block, preferred_element_type=_F32)
            a_blocks = [None] * nb
            for c in reversed(range(nb)):
                cols = slice(c * V7X_LANES, (c + 1) * V7X_LANES)
                w_c = within[(c // 2) * tq:(c // 2 + 1) * tq,
                             (c % 2) * V7X_LANES:(c % 2 + 1) * V7X_LANES]
                log_a = z[:, cols] - sp[:, cols] - w_c - run
                a_blocks[c] = jnp.where(causal[:, cols], jnp.exp(log_a), 0.0).astype(_BF)
                run = run + jnp.sum(spm[:, cols], axis=1, keepdims=True)
            a = jnp.concatenate(a_blocks, axis=1)
            acc = acc + jnp.dot(a, vblk, preferred_element_type=_F32)
            new += [run, acc]
        return tuple(new)

    zero = jnp.zeros((tq, V7X_LANES), _F32)
    zero_col = jnp.zeros((tq, 1), _F32)
    res = lax.fori_loop(0, n_groups, body, (zero_col, zero, zero_col, zero))
    o_ref[...] = jnp.where(in_head[0], res[1], res[3]).astype(_BF)


def _sb_prompt(q_bf, k_bf, v_bf, sb_bias, b, s):
    tq, kg = min(ATT_TQ, s), min(ATT_KG, s)
    q3, k3, v3 = (a.reshape(b, s, SB_WIDTH) for a in (q_bf, k_bf, v_bf))
    kv_spec = pl.BlockSpec((None, s, V7X_LANES), lambda bi, p, i, bias: (bi, 0, p))
    q_spec = pl.BlockSpec((None, tq, V7X_LANES), lambda bi, p, i, bias: (bi, i, p))
    out = pl.pallas_call(
        functools.partial(_sb_prompt_kernel, tq=tq, kg=kg),
        grid_spec=pltpu.PrefetchScalarGridSpec(
            num_scalar_prefetch=1,
            grid=(b, SB_WIDTH // V7X_LANES, s // tq),
            in_specs=[q_spec, kv_spec, kv_spec],
            out_specs=q_spec),
        out_shape=jax.ShapeDtypeStruct((b, s, SB_WIDTH), _BF),
        compiler_params=_cparams(3, 32),
        name="sb_attention_prompt",
    )(sb_bias.astype(_F32), q3, k3, v3)
    return out.reshape(b * s, SB_WIDTH)


def _sb_sample_kernel(pt_ref, q_ref, kn_ref, vn_ref, bias_ref, *refs, page, pps, past_len):
    kc_refs, vc_refs = refs[:pps], refs[pps:2 * pps]
    o_ref, acc_ref, run_ref, onew_ref = refs[2 * pps:]
    j = pl.program_id(1)
    n_steps = pl.num_programs(1)
    zeros_pad = jnp.zeros((V7X_LANES - SB_HEADS, SB_HEAD_DIM), _F32)
    q_pad = jnp.concatenate([q_ref[...], zeros_pad], axis=0).astype(_BF)
    nt = (((1,), (1,)), ((), ()))
    bias = bias_ref[...]
    hrow = lax.broadcasted_iota(jnp.int32, (SB_HEADS, V7X_LANES), 0)
    hlane = lax.broadcasted_iota(jnp.int32, (SB_HEADS, V7X_LANES), 1)

    @pl.when(j == 0)
    def _():
        zn = lax.dot_general(kn_ref[...].astype(_BF), q_pad, nt, preferred_element_type=_F32)
        z_new = jnp.sum(jnp.where(hrow == hlane, zn, 0.0), axis=0, keepdims=True) + bias
        k_pos = past_len + lax.broadcasted_iota(jnp.int32, (1, V7X_LANES), 0)
        q_pos = past_len + lax.broadcasted_iota(jnp.int32, (1, V7X_LANES), 0)
        causal = k_pos < q_pos
        sp_new = _softplus(z_new)
        a_new = jnp.where(causal, jnp.exp(z_new - sp_new), 0.0)
        run_ref[...] = jnp.where(causal, sp_new, 0.0)
        a_col = jnp.sum(jnp.where(hrow == hlane, jnp.broadcast_to(a_new, (SB_HEADS, V7X_LANES)), 0.0),
                        axis=1, keepdims=True)
        onew_ref[...] = a_col.astype(_BF).astype(_F32) * vn_ref[...].astype(_BF).astype(_F32)
        acc_ref[...] = jnp.zeros_like(acc_ref)

    n_keys = pps * page
    lane = lax.broadcasted_iota(jnp.int32, (n_keys, V7X_LANES), 1)
    z = jnp.zeros((n_keys, V7X_LANES), _F32)
    k_heads = [pltpu.einshape("shd->hsd", kc[...]) for kc in kc_refs]
    v_heads = [pltpu.einshape("shd->hsd", vc[...]) for vc in vc_refs]
    for h in range(SB_HEADS):
        k_h = jnp.concatenate([kh[h] for kh in k_heads], axis=0).astype(_BF)
        z_all = lax.dot_general(k_h, q_pad, nt, preferred_element_type=_F32)
        z = jnp.where(lane == h, z_all, z)
    z = z + bias
    sp = _softplus(z)
    ur = lax.broadcasted_iota(jnp.int32, (page, page), 0)
    uc = lax.broadcasted_iota(jnp.int32, (page, page), 1)
    later = jnp.where(uc > ur, 1.0, 0.0).astype(_BF)
    run = run_ref[...]
    a_t = [None] * pps
    for m in reversed(range(pps)):
        rows = slice(m * page, (m + 1) * page)
        sp_m = sp[rows]
        hi = sp_m.astype(_BF)
        lo = (sp_m - hi.astype(_F32)).astype(_BF)
        within = (jnp.dot(later, hi, preferred_element_type=_F32)
                  + jnp.dot(later, lo, preferred_element_type=_F32))
        a_m = jnp.exp(z[rows] - sp_m - within - run)
        a_t[m] = a_m.T[:SB_HEADS].astype(_BF)
        run = run + jnp.sum(sp_m, axis=0, keepdims=True)
    run_ref[...] = run
    a_cat = jnp.concatenate(a_t, axis=1)
    for h in range(SB_HEADS):
        v_h = jnp.concatenate([vh[h] for vh in v_heads], axis=0).astype(_BF)
        acc_ref[h] += jnp.dot(a_cat, v_h, preferred_element_type=_F32)

    @pl.when(j == n_steps - 1)
    def _():
        for h in range(SB_HEADS):
            o_ref[h:h + 1, :] = acc_ref[h, h:h + 1, :] + onew_ref[h:h + 1, :]


def _sb_sample(q, k_new, v_new, cache_k, cache_v, page_table, sb_bias):
    bd = q.shape[0]
    n_pages = page_table.shape[1]
    page = cache_k.shape[1]
    pps = max(p for p in range(1, ATT_PAGES + 1) if n_pages % p == 0)
    heads = pl.BlockSpec((None, SB_HEADS, SB_HEAD_DIM), lambda b, j, pt: (b, 0, 0))

    def page_spec(m):
        return pl.BlockSpec(
            (None, page, SB_HEADS, SB_HEAD_DIM),
            lambda b, j, pt, m=m: (pt[b * n_pages + n_pages - (j + 1) * pps + m], 0, 0, 0))

    h3 = lambda a: a.reshape(bd, SB_HEADS, SB_HEAD_DIM)
    bias_row = jnp.zeros((1, V7X_LANES), _F32).at[0, :SB_HEADS].set(sb_bias.astype(_F32))
    out = pl.pallas_call(
        functools.partial(_sb_sample_kernel, page=page, pps=pps, past_len=n_pages * page),
        grid_spec=pltpu.PrefetchScalarGridSpec(
            num_scalar_prefetch=1,
            grid=(bd, n_pages // pps),
            in_specs=([heads, heads, heads, pl.BlockSpec((1, V7X_LANES), lambda b, j, pt: (0, 0))]
                      + [page_spec(m) for m in range(pps)] * 2),
            out_specs=heads,
            scratch_shapes=[pltpu.VMEM((SB_HEADS, SB_HEADS, SB_HEAD_DIM), _F32),
                            pltpu.VMEM((1, V7X_LANES), _F32),
                            pltpu.VMEM((SB_HEADS, SB_HEAD_DIM), _F32)]),
        out_shape=jax.ShapeDtypeStruct((bd, SB_HEADS, SB_HEAD_DIM), _F32),
        compiler_params=_cparams(2, 40),
        name="sb_attention_sample",
    )(page_table.reshape(-1).astype(jnp.int32), h3(q), h3(k_new), h3(v_new), bias_row,
      *([cache_k] * pps), *([cache_v] * pps))
    return out.reshape(bd, SB_WIDTH)


def _sgu_prompt_kernel(sv_ref, u_ref, w_ref, bias_ref, g_ref, b_ref, o_ref, *, n_chunks):
    r = lax.broadcasted_iota(jnp.int32, (CHUNK, CHUNK), 0)
    c = lax.broadcasted_iota(jnp.int32, (CHUNK, CHUNK), 1)
    lower = r >= c
    for ch in range(n_chunks):
        rows = slice(ch * CHUNK, (ch + 1) * CHUNK)
        svn = _layer_norm(sv_ref[rows, :], g_ref[...], b_ref[...]).astype(_BF)
        for g in range(SGU_GROUPS):
            cols = slice(g * SGU_GROUP_DIM, (g + 1) * SGU_GROUP_DIM)
            w = jnp.where(lower, w_ref[g], 0.0).astype(_BF)
            mixed = jnp.dot(w, svn[:, cols], preferred_element_type=_F32) + bias_ref[:, cols]
            o_ref[rows, cols] = (u_ref[rows, cols] * mixed).astype(_BF)


def _sgu_prompt(sv, u, sgu_w, sgu_b, ln_g, ln_b, tm):
    m = sv.shape[0]
    bias_full = jnp.repeat(sgu_b.T, SGU_GROUP_DIM, axis=1)
    blk = pl.BlockSpec((tm, SGU_WIDTH), lambda i: (i, 0))
    full = lambda shape: pl.BlockSpec(shape, lambda i: (0,) * len(shape))
    return pl.pallas_call(
        functools.partial(_sgu_prompt_kernel, n_chunks=tm // CHUNK),
        grid=(m // tm,),
        in_specs=[blk, blk, full((SGU_GROUPS, CHUNK, CHUNK)), full((CHUNK, SGU_WIDTH)),
                  full((1, SGU_WIDTH)), full((1, SGU_WIDTH))],
        out_specs=blk,
        out_shape=jax.ShapeDtypeStruct((m, SGU_WIDTH), _BF),
        compiler_params=_cparams(1, 32),
        name="sgu_prompt",
    )(sv, u, sgu_w, bias_full, ln_g.reshape(1, -1), ln_b.reshape(1, -1))


def _sgu_sample_kernel(sv_ref, u_ref, w0_ref, b0_ref, g_ref, b_ref, svn_ref, o_ref):
    svn = _layer_norm(sv_ref[...], g_ref[...], b_ref[...])
    svn_ref[...] = svn
    w0 = w0_ref[...].astype(_BF).astype(_F32)
    mixed = w0 * svn.astype(_BF).astype(_F32) + b0_ref[...]
    o_ref[...] = (u_ref[...] * mixed).astype(_BF)


def _sgu_sample(sv, u, sgu_w, sgu_b, ln_g, ln_b):
    m = sv.shape[0]
    w0 = jnp.repeat(sgu_w[:, 0, 0], SGU_GROUP_DIM).reshape(1, SGU_WIDTH)
    b0 = jnp.repeat(sgu_b[:, 0], SGU_GROUP_DIM).reshape(1, SGU_WIDTH)
    blk = pl.BlockSpec((m, SGU_WIDTH), lambda i: (0, 0))
    vec = pl.BlockSpec((1, SGU_WIDTH), lambda i: (0, 0))
    return pl.pallas_call(
        _sgu_sample_kernel,
        grid=(1,),
        in_specs=[blk, blk, vec, vec, vec, vec],
        out_specs=(blk, blk),
        out_shape=(jax.ShapeDtypeStruct((m, SGU_WIDTH), _F32),
                   jax.ShapeDtypeStruct((m, SGU_WIDTH), _BF)),
        compiler_params=_cparams(1, 16),
        name="sgu_sample",
    )(sv, u, w0, b0, ln_g.reshape(1, -1), ln_b.reshape(1, -1))


def _merge_kernel(oa_ref, ob_ref, wa_ref, wb_ref, sga_ref, sgb_ref, m_ref):
    pa = jnp.dot(oa_ref[...], wa_ref[...], preferred_element_type=_F32)
    pb = jnp.dot(ob_ref[...], wb_ref[...], preferred_element_type=_F32)
    m_ref[...] = (sga_ref[...] * pa + sgb_ref[...] * pb).astype(_BF)


def _merge(o_a, o_b, wa_bf, wb_bf, sga, sgb, tm):
    m = o_a.shape[0]
    tn = COL_TILE
    act = pl.BlockSpec((tm, SB_WIDTH), lambda i, j: (i, 0))
    wsp = pl.BlockSpec((SB_WIDTH, tn), lambda i, j: (0, j))
    gsp = pl.BlockSpec((tm, tn), lambda i, j: (i, j))
    return pl.pallas_call(
        _merge_kernel,
        grid=(m // tm, D_MODEL // tn),
        in_specs=[act, act, wsp, wsp, gsp, gsp],
        out_specs=gsp,
        out_shape=jax.ShapeDtypeStruct((m, D_MODEL), _BF),
        compiler_params=_cparams(2, 32),
        name="gated_merge",
    )(o_a, o_b, wa_bf, wb_bf, sga, sgb)


def _out_router_kernel(m_ref, x_ref, wo_ref, g_ref, b_ref, rw_ref, rb_ref,
                       h_ref, hrows_ref, idx_ref, gate_ref, *, alpha):
    tm = m_ref.shape[0]
    y = alpha * x_ref[...] + jnp.dot(m_ref[...], wo_ref[...], preferred_element_type=_F32)
    h = _layer_norm(y, g_ref[...], b_ref[...])
    h_ref[...] = h
    for s in range(h.shape[1] // V7X_LANES):
        hrows_ref[:, s, :] = h[:, s * V7X_LANES:(s + 1) * V7X_LANES]
    logits = jnp.dot(h.astype(_BF), rw_ref[...], preferred_element_type=_F32)
    aff = _sigmoid(logits)
    sel = aff + rb_ref[...]
    e_iota = lax.broadcasted_iota(jnp.int32, (tm, N_EXPERTS), 1).astype(_F32)
    lane = lax.broadcasted_iota(jnp.int32, (tm, V7X_LANES), 1)
    idx_out = jnp.zeros((tm, V7X_LANES), _F32)
    gate_out = jnp.zeros((tm, V7X_LANES), _F32)
    gsum = jnp.zeros((tm, 1), _F32)
    for k in range(TOP_K):
        best = jnp.max(sel, axis=1, keepdims=True)
        ik = jnp.min(jnp.where(sel == best, e_iota, float(N_EXPERTS)), axis=1, keepdims=True)
        hit = e_iota == ik
        gk = jnp.sum(jnp.where(hit, aff, 0.0), axis=1, keepdims=True)
        sel = jnp.where(hit, -jnp.inf, sel)
        idx_out = jnp.where(lane == k, ik, idx_out)
        gate_out = jnp.where(lane == k, gk, gate_out)
        gsum = gsum + gk
    idx_ref[...] = idx_out.astype(jnp.int32)
    gate_ref[...] = ROUTED_SCALE * gate_out / gsum


def _out_router(merged, x, wo_bf, ln_g, ln_b, rw_bf, rbias, alpha, tm):
    m = x.shape[0]
    slabs = D_MODEL // V7X_LANES
    row = lambda w: pl.BlockSpec((tm, w), lambda i: (i, 0))
    full = lambda shape: pl.BlockSpec(shape, lambda i: (0, 0))
    return pl.pallas_call(
        functools.partial(_out_router_kernel, alpha=alpha),
        grid=(m // tm,),
        in_specs=[row(D_MODEL), row(D_MODEL), full((D_MODEL, D_MODEL)), full((1, D_MODEL)),
                  full((1, D_MODEL)), full((D_MODEL, N_EXPERTS)), full((1, N_EXPERTS))],
        out_specs=(row(D_MODEL), pl.BlockSpec((tm, slabs, V7X_LANES), lambda i: (i, 0, 0)),
                   row(V7X_LANES), row(V7X_LANES)),
        out_shape=(jax.ShapeDtypeStruct((m, D_MODEL), _F32),
                   jax.ShapeDtypeStruct((m, slabs, V7X_LANES), _F32),
                   jax.ShapeDtypeStruct((m, V7X_LANES), jnp.int32),
                   jax.ShapeDtypeStruct((m, V7X_LANES), _F32)),
        compiler_params=_cparams(1, 48),
        name="out_proj_router",
    )(merged, x, wo_bf, ln_g.reshape(1, -1), ln_b.reshape(1, -1), rw_bf,
      rbias.astype(_F32).reshape(1, -1))


def _rank_kernel(idx_ref, rank_ref, cnt_ref, carry_ref):
    t = pl.program_id(0)
    tt = idx_ref.shape[0]

    @pl.when(t == 0)
    def _():
        carry_ref[...] = jnp.zeros_like(carry_ref)

    idx = idx_ref[...]
    e_iota = lax.broadcasted_iota(jnp.int32, (tt, V7X_LANES), 1)
    hits = [e_iota == idx[:, k:k + 1] for k in range(TOP_K)]
    member = jnp.zeros((tt, V7X_LANES), _F32)
    for hit in hits:
        member = member + jnp.where(hit, 1.0, 0.0)
    r = lax.broadcasted_iota(jnp.int32, (tt, tt), 0)
    c = lax.broadcasted_iota(jnp.int32, (tt, tt), 1)
    earlier = jnp.where(c < r, 1.0, 0.0).astype(_BF)
    before = jnp.dot(earlier, member.astype(_BF), preferred_element_type=_F32) + carry_ref[0:1, :]
    rank = jnp.zeros((tt, V7X_LANES), _F32)
    for k, hit in enumerate(hits):
        rk = jnp.sum(jnp.where(hit, before, 0.0), axis=1, keepdims=True)
        rank = jnp.where(e_iota == k, rk, rank)
    rank_ref[...] = rank.astype(jnp.int32)
    carry_ref[0:1, :] = carry_ref[0:1, :] + jnp.sum(member, axis=0, keepdims=True)
    cnt_ref[...] = carry_ref[...]


def _ranks(idx_lanes):
    t = idx_lanes.shape[0]
    tt = TOK_TILE
    blk = pl.BlockSpec((tt, V7X_LANES), lambda i: (i, 0))
    return pl.pallas_call(
        _rank_kernel,
        grid=(t // tt,),
        in_specs=[blk],
        out_specs=(blk, pl.BlockSpec((8, V7X_LANES), lambda i: (0, 0))),
        out_shape=(jax.ShapeDtypeStruct((t, V7X_LANES), jnp.int32),
                   jax.ShapeDtypeStruct((8, V7X_LANES), _F32)),
        scratch_shapes=[pltpu.VMEM((8, V7X_LANES), _F32)],
        compiler_params=_cparams(1, 16),
        name="expert_ranks",
    )(idx_lanes)


def _row_copy(h_hbm, xbuf, sem, tok, r):
    return pltpu.make_async_copy(h_hbm.at[tok], xbuf.at[r], sem)


def _moe_kernel(te_ref, nu_ref, tokc_ref, tokn_ref, h_hbm, wg_ref, wu_ref, wd_ref, o_ref,
                xbuf, accg, accu, act, sem, *, tile, n_kc):
    i = pl.program_id(0)
    s = pl.program_id(1)
    n_used = nu_ref[0]
    live = i < n_used

    def gather(tok_ref):
        def body(r, carry):
            _row_copy(h_hbm, xbuf, sem, tok_ref[0, 0, r], r).start()
            return carry
        lax.fori_loop(0, tile, body, 0, unroll=8)

    @pl.when((i == 0) & (s == 0))
    def _():
        gather(tokc_ref)

    @pl.when(live & (s == 0))
    def _():
        def body(r, carry):
            _row_copy(h_hbm, xbuf, sem, 0, r).wait()
            return carry
        lax.fori_loop(0, tile, body, 0, unroll=8)

    slabs = xbuf.shape[1] // n_kc
    for c in range(n_kc):
        @pl.when(live & (s == c))
        def _(c=c):
            xk = jnp.concatenate([xbuf[:, c * slabs + t, :] for t in range(slabs)],
                                 axis=1).astype(_BF)
            g = jnp.dot(xk, wg_ref[...].astype(_BF), preferred_element_type=_F32)
            u = jnp.dot(xk, wu_ref[...].astype(_BF), preferred_element_type=_F32)
            if c > 0:
                g = g + accg[...]
                u = u + accu[...]
            if c < n_kc - 1:
                accg[...] = g
                accu[...] = u
            else:
                act[...] = (g * _sigmoid(g) * u).astype(_BF)

    @pl.when(live & (s == n_kc) & (i + 1 < n_used))
    def _():
        gather(tokn_ref)

    @pl.when(live & (s >= n_kc))
    def _():
        res = jnp.dot(act[...], wd_ref[...].astype(_BF), preferred_element_type=_F32)
        for t in range(o_ref.shape[1]):
            o_ref[:, t, :] = res[:, t * V7X_LANES:(t + 1) * V7X_LANES]

    @pl.when(jnp.logical_not(live) & (s >= n_kc))
    def _():
        o_ref[...] = jnp.zeros_like(o_ref)


def _routed_experts(h_rows, tok_of_row, tile_expert, n_used, w_gate, w_up, w_down):
    tile = MOE_TILE
    n_tiles = tile_expert.shape[0]
    d, f = w_gate.shape[1:]
    n_kc, n_nc = d // MOE_KC, d // MOE_NC
    slabs = d // V7X_LANES
    tok3 = tok_of_row.reshape(n_tiles, 1, tile)

    def k_idx(i, s, te, nu):
        return jnp.where(i < nu[0], jnp.minimum(s, n_kc - 1), n_kc - 1)

    def n_idx(s):
        return jnp.clip(s - n_kc, 0, n_nc - 1)

    smem_row = lambda fn: pl.BlockSpec((1, 1, tile), fn, memory_space=pltpu.SMEM)
    in_specs = [
        smem_row(lambda i, s, te, nu: (i, 0, 0)),
        smem_row(lambda i, s, te, nu: (jnp.minimum(i + 1, n_tiles - 1), 0, 0)),
        pl.BlockSpec(memory_space=pl.ANY),
        pl.BlockSpec((None, MOE_KC, f), lambda i, s, te, nu: (te[i], k_idx(i, s, te, nu), 0)),
        pl.BlockSpec((None, MOE_KC, f), lambda i, s, te, nu: (te[i], k_idx(i, s, te, nu), 0)),
        pl.BlockSpec((None, f, MOE_NC), lambda i, s, te, nu: (te[i], 0, n_idx(s))),
    ]
    return pl.pallas_call(
        functools.partial(_moe_kernel, tile=tile, n_kc=n_kc),
        grid_spec=pltpu.PrefetchScalarGridSpec(
            num_scalar_prefetch=2,
            grid=(n_tiles, n_kc + n_nc),
            in_specs=in_specs,
            out_specs=pl.BlockSpec((tile, MOE_NC // V7X_LANES, V7X_LANES),
                                   lambda i, s, te, nu: (i, n_idx(s), 0)),
            scratch_shapes=[pltpu.VMEM((tile, slabs, V7X_LANES), _F32), pltpu.VMEM((tile, f), _F32),
                            pltpu.VMEM((tile, f), _F32), pltpu.VMEM((tile, f), _BF),
                            pltpu.SemaphoreType.DMA(())]),
        out_shape=jax.ShapeDtypeStruct((n_tiles * tile, slabs, V7X_LANES), _F32),
        compiler_params=_cparams(2, 56),
        name="routed_experts",
    )(tile_expert, n_used, tok3, tok3, h_rows, w_gate, w_up, w_down)


def _shared_kernel(x_ref, wg_ref, wu_ref, wd_ref, o_ref, accg, accu, act, *, n_kc):
    s = pl.program_id(1)
    for c in range(n_kc):
        @pl.when(s == c)
        def _(c=c):
            xk = x_ref[...].astype(_BF)
            g = jnp.dot(xk, wg_ref[...], preferred_element_type=_F32)
            u = jnp.dot(xk, wu_ref[...], preferred_element_type=_F32)
            if c > 0:
                g = g + accg[...]
                u = u + accu[...]
            if c < n_kc - 1:
                accg[...] = g
                accu[...] = u
            else:
                act[...] = (g * _sigmoid(g) * u).astype(_BF)

    @pl.when(s >= n_kc)
    def _():
        o_ref[...] = jnp.dot(act[...], wd_ref[...], preferred_element_type=_F32)


def _shared_expert(h, wg_bf, wu_bf, wd_bf, tm):
    m, d = h.shape
    f = wg_bf.shape[1]
    n_kc, n_nc = d // MOE_KC, d // SHARED_NC
    k_idx = lambda s: jnp.minimum(s, n_kc - 1)
    n_idx = lambda s: jnp.clip(s - n_kc, 0, n_nc - 1)
    return pl.pallas_call(
        functools.partial(_shared_kernel, n_kc=n_kc),
        grid=(m // tm, n_kc + n_nc),
        in_specs=[pl.BlockSpec((tm, MOE_KC), lambda i, s: (i, k_idx(s))),
                  pl.BlockSpec((MOE_KC, f), lambda i, s: (k_idx(s), 0)),
                  pl.BlockSpec((MOE_KC, f), lambda i, s: (k_idx(s), 0)),
                  pl.BlockSpec((f, SHARED_NC), lambda i, s: (0, n_idx(s)))],
        out_specs=pl.BlockSpec((tm, SHARED_NC), lambda i, s: (i, n_idx(s))),
        out_shape=jax.ShapeDtypeStruct((m, d), _F32),
        scratch_shapes=[pltpu.VMEM((tm, f), _F32), pltpu.VMEM((tm, f), _F32),
                        pltpu.VMEM((tm, f), _BF)],
        compiler_params=_cparams(2, 40),
        name="shared_expert",
    )(h, wg_bf, wu_bf, wd_bf)


def _combine_kernel(destc_ref, destn_ref, gate_ref, h_ref, sh_ref, outs_hbm, g_ref, b_ref,
                    yp_ref, ys_ref, gbuf, pre_ref, sem, *, alpha, n_prompt_tiles):
    i = pl.program_id(0)
    n_steps = pl.num_programs(0)
    tt = h_ref.shape[0]
    slabs = gbuf.shape[3]
    slot = lax.rem(i, 2)

    def copy(src_row, sl, k, r):
        return pltpu.make_async_copy(outs_hbm.at[src_row], gbuf.at[sl, k, r], sem.at[sl])

    def issue(dest_ref, sl):
        def body(r, carry):
            for k in range(TOP_K):
                copy(dest_ref[0, 0, r * TOP_K + k], sl, k, r).start()
            return carry
        lax.fori_loop(0, tt, body, 0, unroll=2)

    @pl.when(i == 0)
    def _():
        issue(destc_ref, 0)

    @pl.when(i + 1 < n_steps)
    def _():
        issue(destn_ref, 1 - slot)

    def drain(r, carry):
        for k in range(TOP_K):
            copy(0, slot, k, r).wait()
        return carry

    lax.fori_loop(0, tt, drain, 0, unroll=2)
    gate = gate_ref[...]
    gate_b = [jnp.broadcast_to(gate[:, k:k + 1], (tt, V7X_LANES)) for k in range(TOP_K)]
    for s in range(slabs):
        cols = slice(s * V7X_LANES, (s + 1) * V7X_LANES)
        acc = alpha * h_ref[:, cols] + sh_ref[:, cols]
        for k in range(TOP_K):
            acc = acc + gate_b[k] * gbuf[slot, k, :, s, :]
        pre_ref[:, cols] = acc
    y = _layer_norm(pre_ref[...], g_ref[...], b_ref[...])

    @pl.when(i < n_prompt_tiles)
    def _():
        yp_ref[...] = y

    @pl.when(i >= n_prompt_tiles)
    def _():
        ys_ref[...] = y


def _combine(dest, gate_lanes, h_all, shared_all, outs, ln_g, ln_b, alpha, n_prompt, n_sample):
    tt = TOK_TILE
    t_all, d = h_all.shape
    n_pt, n_st = n_prompt // tt, n_sample // tt
    n_steps = n_pt + n_st
    dest3 = dest.reshape(n_steps, 1, tt * TOP_K)
    row = lambda w: pl.BlockSpec((tt, w), lambda i: (i, 0))
    vec = pl.BlockSpec((1, d), lambda i: (0, 0))
    dest_spec = lambda fn: pl.BlockSpec((1, 1, tt * TOP_K), fn, memory_space=pltpu.SMEM)
    return pl.pallas_call(
        functools.partial(_combine_kernel, alpha=alpha, n_prompt_tiles=n_pt),
        grid=(n_steps,),
        in_specs=[dest_spec(lambda i: (i, 0, 0)),
                  dest_spec(lambda i: (jnp.minimum(i + 1, n_steps - 1), 0, 0)),
                  row(V7X_LANES), row(d), row(d), pl.BlockSpec(memory_space=pl.ANY), vec, vec],
        out_specs=(pl.BlockSpec((tt, d), lambda i: (jnp.minimum(i, n_pt - 1), 0)),
                   pl.BlockSpec((tt, d), lambda i: (jnp.clip(i - n_pt, 0, n_st - 1), 0))),
        out_shape=(jax.ShapeDtypeStruct((n_prompt, d), _F32),
                   jax.ShapeDtypeStruct((n_sample, d), _F32)),
        scratch_shapes=[pltpu.VMEM((2, TOP_K, tt, d // V7X_LANES, V7X_LANES), _F32),
                        pltpu.VMEM((tt, d), _F32), pltpu.SemaphoreType.DMA((2,))],
        compiler_params=_cparams(1, 40),
        name="moe_combine",
    )(dest3, dest3, gate_lanes, h_all, shared_all, outs, ln_g.reshape(1, -1), ln_b.reshape(1, -1))


def _routing_tables(idx_lanes, rank_lanes, counts_row):
    n_tok = idx_lanes.shape[0]
    n_assign = n_tok * TOP_K
    n_tiles = -(-n_assign // MOE_TILE) + N_EXPERTS
    idx = idx_lanes[:, :TOP_K]
    counts = counts_row[0, :N_EXPERTS].astype(jnp.int32)
    padded = (counts + MOE_TILE - 1) // MOE_TILE * MOE_TILE
    pad_end = jnp.cumsum(padded)
    pad_start = pad_end - padded
    dest = pad_start[idx] + rank_lanes[:, :TOP_K]
    n_used = pad_end[-1] // MOE_TILE
    tiles = jnp.minimum(jnp.arange(n_tiles, dtype=jnp.int32), n_used - 1)
    ends_before = (pad_end[None, :] <= (tiles * MOE_TILE)[:, None]).astype(jnp.int32)
    tile_expert = jnp.minimum(jnp.sum(ends_before, axis=1), N_EXPERTS - 1)
    tok_of_row = jnp.zeros((n_tiles * MOE_TILE,), jnp.int32).at[dest.reshape(-1)].set(
        jnp.arange(n_assign, dtype=jnp.int32) // TOP_K, unique_indices=True)
    return (dest.astype(jnp.int32), tok_of_row, tile_expert.astype(jnp.int32),
            n_used.astype(jnp.int32).reshape(1))


def _mixers_to_router(x2, o_a, o_b, sga, sgb, wts, alpha, tm):
    merged = _merge(o_a, o_b, wts["w_proj_a"], wts["w_proj_b"], sga, sgb, tm)
    return _out_router(merged, x2, wts["w_out"], wts["ln1_g"], wts["ln1_b"],
                       wts["router_w"], wts["router_bias"], alpha, min(tm, ROUTER_TILE))


def _layer(yp, ys, cache_k, cache_v, page_table, wts, alpha):
    b, s, d = yp.shape
    bd, t, _ = ys.shape
    n_p, n_s = b * s, bd * t
    xp, xs = yp.reshape(n_p, d), ys.reshape(n_s, d)

    q, k, kb, v, vb, u, sv, sga, sgb = _in_projection(xp, wts["w_in"], ROW_TILE)
    o_a = _sb_prompt(q, kb, vb, wts["sb_bias"], b, s)
    o_b = _sgu_prompt(sv, u, wts["sgu_w"], wts["sgu_b"], wts["sgu_ln_g"], wts["sgu_ln_b"], ROW_TILE)
    h_p, hr_p, idx_p, gate_p = _mixers_to_router(xp, o_a, o_b, sga, sgb, wts, alpha, ROW_TILE)
    k_p = k.reshape(b, s, SB_HEADS, SB_HEAD_DIM)
    v_p = v.reshape(b, s, SB_HEADS, SB_HEAD_DIM)

    q, k, kb, v, vb, u, sv, sga, sgb = _in_projection(xs, wts["w_in"], n_s)
    o_a = _sb_sample(q.astype(_F32), k, v, cache_k, cache_v, page_table, wts["sb_bias"])
    svn, o_b = _sgu_sample(sv, u, wts["sgu_w"], wts["sgu_b"], wts["sgu_ln_g"], wts["sgu_ln_b"])
    h_s, hr_s, idx_s, gate_s = _mixers_to_router(xs, o_a.astype(_BF), o_b, sga, sgb, wts, alpha, n_s)
    k_s = k.reshape(bd, t, SB_HEADS, SB_HEAD_DIM)
    v_s = v.reshape(bd, t, SB_HEADS, SB_HEAD_DIM)

    h_all = jnp.concatenate([h_p, h_s], axis=0)
    h_rows = jnp.concatenate([hr_p, hr_s], axis=0)
    idx_all = jnp.concatenate([idx_p, idx_s], axis=0)
    gate_all = jnp.concatenate([gate_p, gate_s], axis=0)
    rank, counts = _ranks(idx_all)
    dest, tok_of_row, tile_expert, n_used = _routing_tables(idx_all, rank, counts)
    outs = _routed_experts(h_rows, tok_of_row, tile_expert, n_used,
                           wts["exp_w_gate"], wts["exp_w_up"], wts["exp_w_down"])
    sh_p = _shared_expert(h_p, wts["sh_w_gate"], wts["sh_w_up"], wts["sh_w_down"], ROW_TILE)
    sh_s = _shared_expert(h_s, wts["sh_w_gate"], wts["sh_w_up"], wts["sh_w_down"], n_s)
    shared_all = jnp.concatenate([sh_p, sh_s], axis=0)
    yp2, ys2 = _combine(dest, gate_all, h_all, shared_all, outs, wts["ln2_g"], wts["ln2_b"],
                        alpha, n_p, n_s)
    return (yp2.reshape(b, s, d), ys2.reshape(bd, t, d), k_p, v_p, k_s, v_s,
            svn.reshape(bd, t, SGU_WIDTH))


def kernel(x_prompt, x_sample, cache_k, cache_v, page_table, w_in, sb_bias, sgu_ln_g, sgu_ln_b, sgu_w, sgu_b, w_proj_a, w_proj_b, w_out, ln1_g, ln1_b, router_w, router_bias, exp_w_gate, exp_w_up, exp_w_down, sh_w_gate, sh_w_up, sh_w_down, ln2_g, ln2_b):
    depth = w_in.shape[0]
    alpha = (2.0 * depth) ** 0.25
    yp, ys = x_prompt, x_sample
    kp, vp, ksm, vsm, svs = [], [], [], [], []
    for l in range(depth):
        wts = {
            "w_in": w_in[l].astype(_BF), "sb_bias": sb_bias[l],
            "sgu_ln_g": sgu_ln_g[l], "sgu_ln_b": sgu_ln_b[l], "sgu_w": sgu_w[l], "sgu_b": sgu_b[l],
            "w_proj_a": w_proj_a[l].astype(_BF), "w_proj_b": w_proj_b[l].astype(_BF),
            "w_out": w_out[l].astype(_BF), "ln1_g": ln1_g[l], "ln1_b": ln1_b[l],
            "router_w": router_w[l].astype(_BF), "router_bias": router_bias[l],
            "exp_w_gate": exp_w_gate[l], "exp_w_up": exp_w_up[l], "exp_w_down": exp_w_down[l],
            "sh_w_gate": sh_w_gate[l].astype(_BF), "sh_w_up": sh_w_up[l].astype(_BF),
            "sh_w_down": sh_w_down[l].astype(_BF), "ln2_g": ln2_g[l], "ln2_b": ln2_b[l],
        }
        yp, ys, k_p, v_p, k_s, v_s, svn = _layer(yp, ys, cache_k[l], cache_v[l], page_table,
                                                 wts, alpha)
        kp.append(k_p)
        vp.append(v_p)
        ksm.append(k_s)
        vsm.append(v_s)
        svs.append(svn)
    return (yp, ys, jnp.stack(kp), jnp.stack(vp), jnp.stack(ksm), jnp.stack(vsm), jnp.stack(svs))
```

```python
import functools

import jax
import jax.numpy as jnp
from jax import lax
from jax.experimental import pallas as pl
from jax.experimental.pallas import tpu as pltpu

D_MODEL = 2048
SB_HEADS = 16
SB_HEAD_DIM = 64
SB_WIDTH = SB_HEADS * SB_HEAD_DIM
SGU_GROUPS = 8
SGU_GROUP_DIM = 128
SGU_WIDTH = SGU_GROUPS * SGU_GROUP_DIM
CHUNK = 128
N_EXPERTS = 64
TOP_K = 6
D_EXPERT = 1408
ROUTED_SCALE = 2.5
LN_EPS = 1e-5
SB_SCALE = SB_HEAD_DIM ** -0.5

V7X_LANES = 128
V7X_VMEM_BYTES = 64 * 2 ** 20

ROW_TILE = 512
COL_TILE = 512
ATT_TQ = 256
ATT_KG = 512
ATT_PAGES = 4
MOE_TILE = 512
MOE_KC = 512
MOE_NC = 1024
SHARED_NC = 512
TOK_TILE = 128
ROUTER_TILE = 256
SLAB_PITCH = 24

_BF = jnp.bfloat16
_F32 = jnp.float32


def _cparams(n_axes, vmem_mib):
    return pltpu.CompilerParams(
        dimension_semantics=("arbitrary",) * n_axes,
        vmem_limit_bytes=min(vmem_mib * 2 ** 20, V7X_VMEM_BYTES - 6 * 2 ** 20))


def _layer_norm(x, g, b):
    mu = jnp.mean(x, axis=-1, keepdims=True)
    xc = x - mu
    var = jnp.mean(xc * xc, axis=-1, keepdims=True)
    return xc * lax.rsqrt(var + LN_EPS) * g + b


def _gelu(x):
    return 0.5 * x * (1.0 + lax.erf(x * 0.7071067811865476))


def _sigmoid(x):
    return 1.0 / (1.0 + jnp.exp(-x))


def _softplus(z):
    return jnp.maximum(z, 0.0) + jnp.log(1.0 + jnp.exp(-jnp.abs(z)))


def _dot_halves(x, w):
    half = w.shape[1] // 2
    return jnp.concatenate([jnp.dot(x, w[:, :half], preferred_element_type=_F32),
                            jnp.dot(x, w[:, half:], preferred_element_type=_F32)], axis=1)


def _split_dot(x, u):
    hi = x.astype(_BF)
    lo = (x - hi.astype(_F32)).astype(_BF)
    return (jnp.dot(hi, u, preferred_element_type=_F32)
            + jnp.dot(lo, u, preferred_element_type=_F32))


def _in_sections(tn):
    widths = (SB_WIDTH, SB_WIDTH, SB_WIDTH, SGU_WIDTH, SGU_WIDTH, D_MODEL, D_MODEL)
    starts, s = [], 0
    for w in widths:
        starts.append(s // tn)
        s += w
    return starts, [w // tn for w in widths]


def _in_proj_kernel(x_ref, w_ref, q_ref, k_ref, kb_ref, v_ref, vb_ref, u_ref, sv_ref,
                    ga_ref, gb_ref, xb_ref, *, starts, counts):
    j = pl.program_id(1)

    @pl.when(j == 0)
    def _():
        xb_ref[...] = x_ref[...].astype(_BF)

    acc = _dot_halves(xb_ref[...], w_ref[...])

    def section(n):
        return (j >= starts[n]) & (j < starts[n] + counts[n])

    @pl.when(section(0))
    def _():
        q_ref[...] = (acc * SB_SCALE).astype(_BF)

    @pl.when(section(1))
    def _():
        k_ref[...] = acc
        kb_ref[...] = acc.astype(_BF)

    @pl.when(section(2))
    def _():
        v_ref[...] = acc
        vb_ref[...] = acc.astype(_BF)

    @pl.when(section(3))
    def _():
        u_ref[...] = _gelu(acc)

    @pl.when(section(4))
    def _():
        sv_ref[...] = _gelu(acc)

    @pl.when(section(5))
    def _():
        ga_ref[...] = _sigmoid(acc)

    @pl.when(section(6))
    def _():
        gb_ref[...] = _sigmoid(acc)


def _in_projection(x, w_bf, tm):
    m, d = x.shape
    tn = COL_TILE
    starts, counts = _in_sections(tn)
    n_tiles = w_bf.shape[1] // tn

    def out_spec(n):
        return pl.BlockSpec(
            (tm, tn), lambda i, j, n=n: (i, jnp.clip(j - starts[n], 0, counts[n] - 1)))

    shp = lambda w, dt: jax.ShapeDtypeStruct((m, w), dt)
    out_shape = (shp(SB_WIDTH, _BF), shp(SB_WIDTH, _F32), shp(SB_WIDTH, _BF),
                 shp(SB_WIDTH, _F32), shp(SB_WIDTH, _BF), shp(SGU_WIDTH, _F32),
                 shp(SGU_WIDTH, _F32), shp(D_MODEL, _F32), shp(D_MODEL, _F32))
    out_specs = (out_spec(0), out_spec(1), out_spec(1), out_spec(2), out_spec(2),
                 out_spec(3), out_spec(4), out_spec(5), out_spec(6))
    return pl.pallas_call(
        functools.partial(_in_proj_kernel, starts=starts, counts=counts),
        grid=(m // tm, n_tiles),
        in_specs=[pl.BlockSpec((tm, d), lambda i, j: (i, 0)),
                  pl.BlockSpec((d, tn), lambda i, j: (0, j))],
        out_specs=out_specs,
        out_shape=out_shape,
        scratch_shapes=[pltpu.VMEM((tm, d), _BF)],
        compiler_params=_cparams(2, 40),
        name="in_projection",
    )(x, w_bf)


def _sb_prompt_kernel(bias_ref, q_ref, k_ref, v_ref, o_ref, *, tq, kg):
    p = pl.program_id(1)
    i = pl.program_id(2)
    nb = kg // V7X_LANES
    q2 = q_ref[...].astype(_F32)
    lane = lax.broadcasted_iota(jnp.int32, (tq, V7X_LANES), 1)
    row_pos = i * tq + lax.broadcasted_iota(jnp.int32, (tq, kg), 0)
    col = lax.broadcasted_iota(jnp.int32, (tq, kg), 1)
    ur = lax.broadcasted_iota(jnp.int32, (2 * V7X_LANES, 2 * V7X_LANES), 0)
    uc = lax.broadcasted_iota(jnp.int32, (2 * V7X_LANES, 2 * V7X_LANES), 1)
    later_in_block = jnp.where(((ur >= V7X_LANES) == (uc >= V7X_LANES)) & (ur > uc),
                               1.0, 0.0).astype(_BF)
    n_groups = ((i + 1) * tq + kg - 1) // kg
    in_head = [(lane >= hh * SB_HEAD_DIM) & (lane < (hh + 1) * SB_HEAD_DIM) for hh in range(2)]
    qm = [jnp.where(in_head[hh], q2, 0.0).astype(_BF) for hh in range(2)]
    bias = [bias_ref[2 * p + hh] for hh in range(2)]

    def body(t, carry):
        off = pl.multiple_of((n_groups - 1 - t) * kg, kg)
        kblk = k_ref[pl.ds(off, kg), :]
        vblk = v_ref[pl.ds(off, kg), :]
        causal = (off + col) < row_pos
        new = []
        for hh in range(2):
            run, acc = carry[2 * hh], carry[2 * hh + 1]
            z = lax.dot_general(qm[hh], kblk, (((1,), (1,)), ((), ())),
                                preferred_element_type=_F32) + bias[hh]
            sp = _softplus(z)
            spm = jnp.where(causal, sp, 0.0)
            spb = spm.astype(_BF)
            pairs = jnp.concatenate(
                [spb[:, g * 2 * V7X_LANES:(g + 1) * 2 * V7X_LANES] for g in range(nb // 2)], axis=0)
            within = jnp.dot(pairs, later_in_block, preferred_element_type=_F32)
            a_blocks = [None] * nb
            for c in reversed(range(nb)):
                cols = slice(c * V7X_LANES, (c + 1) * V7X_LANES)
                w_c = within[(c // 2) * tq:(c // 2 + 1) * tq,
                             (c % 2) * V7X_LANES:(c % 2 + 1) * V7X_LANES]
                log_a = z[:, cols] - sp[:, cols] - w_c - run
                a_blocks[c] = jnp.where(causal[:, cols], jnp.exp(log_a), 0.0).astype(_BF)
                run = run + jnp.sum(spm[:, cols], axis=1, keepdims=True)
            a = jnp.concatenate(a_blocks, axis=1)
            acc = acc + jnp.dot(a, vblk, preferred_element_type=_F32)
            new += [run, acc]
        return tuple(new)

    zero = jnp.zeros((tq, V7X_LANES), _F32)
    zero_col = jnp.zeros((tq, 1), _F32)
    res = lax.fori_loop(0, n_groups, body, (zero_col, zero, zero_col, zero))
    o_ref[...] = jnp.where(in_head[0], res[1], res[3]).astype(_BF)


def _sb_prompt(q_bf, k_bf, v_bf, sb_bias, b, s):
    tq, kg = min(ATT_TQ, s), min(ATT_KG, s)
    q3, k3, v3 = (a.reshape(b, s, SB_WIDTH) for a in (q_bf, k_bf, v_bf))
    kv_spec = pl.BlockSpec((None, s, V7X_LANES), lambda bi, p, i, bias: (bi, 0, p))
    q_spec = pl.BlockSpec((None, tq, V7X_LANES), lambda bi, p, i, bias: (bi, i, p))
    out = pl.pallas_call(
        functools.partial(_sb_prompt_kernel, tq=tq, kg=kg),
        grid_spec=pltpu.PrefetchScalarGridSpec(
            num_scalar_prefetch=1,
            grid=(b, SB_WIDTH // V7X_LANES, s // tq),
            in_specs=[q_spec, kv_spec, kv_spec],
            out_specs=q_spec),
        out_shape=jax.ShapeDtypeStruct((b, s, SB_WIDTH), _BF),
        compiler_params=_cparams(3, 32),
        name="sb_attention_prompt",
    )(sb_bias.astype(_F32), q3, k3, v3)
    return out.reshape(b * s, SB_WIDTH)


def _sb_sample_kernel(pt_ref, q_ref, kn_ref, vn_ref, bias_ref, *refs, page, pps, past_len):
    kc_refs, vc_refs = refs[:pps], refs[pps:2 * pps]
    o_ref, acc_ref, run_ref = refs[2 * pps:]
    j = pl.program_id(1)
    n_steps = pl.num_programs(1)
    head = lax.broadcasted_iota(jnp.int32, (SB_HEADS, SB_WIDTH), 0)
    lane = lax.broadcasted_iota(jnp.int32, (SB_HEADS, SB_WIDTH), 1)
    in_head = (lane >= head * SB_HEAD_DIM) & (lane < (head + 1) * SB_HEAD_DIM)
    q_bd = jnp.where(in_head, q_ref[...], 0.0).astype(_BF)
    bias = bias_ref[...]
    nt = (((1,), (1,)), ((), ()))

    @pl.when(j == 0)
    def _():
        k_new = kn_ref[...].astype(_BF).astype(_F32)
        v_new = vn_ref[...].astype(_BF).astype(_F32)
        z_new = jnp.sum(q_bd.astype(_F32) * k_new, axis=1, keepdims=True) + bias
        k_pos = past_len + lax.broadcasted_iota(jnp.int32, (SB_HEADS, 1), 1)
        q_pos = past_len + lax.broadcasted_iota(jnp.int32, (SB_HEADS, 1), 1)
        causal = k_pos < q_pos
        sp_new = _softplus(z_new)
        a_new = jnp.where(causal, jnp.exp(z_new - sp_new), 0.0).astype(_BF).astype(_F32)
        acc_ref[...] = a_new * v_new
        run_ref[...] = jnp.where(causal, sp_new, 0.0)

    ur = lax.broadcasted_iota(jnp.int32, (page, page), 0)
    uc = lax.broadcasted_iota(jnp.int32, (page, page), 1)
    upper = jnp.where(ur > uc, 1.0, 0.0).astype(_BF)
    zs = [jnp.dot(q_bd, kc[...].astype(_BF), preferred_element_type=_F32) + bias
          for kc in kc_refs]
    run = run_ref[...]
    acc = acc_ref[...]
    for m in reversed(range(pps)):
        sp = _softplus(zs[m])
        within = _split_dot(sp, upper)
        a = jnp.exp(zs[m] - sp - within - run).astype(_BF)
        acc = acc + lax.dot_general(a, vc_refs[m][...].astype(_BF), nt,
                                    preferred_element_type=_F32)
        run = run + jnp.sum(sp, axis=1, keepdims=True)
    acc_ref[...] = acc
    run_ref[...] = run

    @pl.when(j == n_steps - 1)
    def _():
        o_ref[...] = jnp.sum(jnp.where(in_head, acc, 0.0), axis=0, keepdims=True)


def _sb_sample(q, k_new, v_new, cache_k, cache_v, page_table, sb_bias):
    bd = q.shape[0]
    n_pages = page_table.shape[1]
    n_pool, page = cache_k.shape[:2]
    pps = max(p for p in range(1, ATT_PAGES + 1) if n_pages % p == 0)
    ck = jnp.transpose(cache_k, (0, 2, 3, 1)).reshape(n_pool, SB_WIDTH, page)
    cv = jnp.transpose(cache_v, (0, 2, 3, 1)).reshape(n_pool, SB_WIDTH, page)
    row = pl.BlockSpec((None, 1, SB_WIDTH), lambda b, j, pt: (b, 0, 0))

    def page_spec(m):
        return pl.BlockSpec(
            (None, SB_WIDTH, page),
            lambda b, j, pt, m=m: (pt[b * n_pages + n_pages - (j + 1) * pps + m], 0, 0))

    r3 = lambda a: a.reshape(bd, 1, SB_WIDTH)
    out = pl.pallas_call(
        functools.partial(_sb_sample_kernel, page=page, pps=pps, past_len=n_pages * page),
        grid_spec=pltpu.PrefetchScalarGridSpec(
            num_scalar_prefetch=1,
            grid=(bd, n_pages // pps),
            in_specs=([row, row, row, pl.BlockSpec((SB_HEADS, 1), lambda b, j, pt: (0, 0))]
                      + [page_spec(m) for m in range(pps)] * 2),
            out_specs=row,
            scratch_shapes=[pltpu.VMEM((SB_HEADS, SB_WIDTH), _F32),
                            pltpu.VMEM((SB_HEADS, 1), _F32)]),
        out_shape=jax.ShapeDtypeStruct((bd, 1, SB_WIDTH), _F32),
        compiler_params=_cparams(2, 32),
        name="sb_attention_sample",
    )(page_table.reshape(-1).astype(jnp.int32), r3(q), r3(k_new), r3(v_new),
      sb_bias.astype(_F32).reshape(SB_HEADS, 1), *([ck] * pps), *([cv] * pps))
    return out.reshape(bd, SB_WIDTH)


def _sgu_prompt_kernel(sv_ref, u_ref, w_ref, bias_ref, g_ref, b_ref, o_ref, *, n_chunks):
    r = lax.broadcasted_iota(jnp.int32, (CHUNK, CHUNK), 0)
    c = lax.broadcasted_iota(jnp.int32, (CHUNK, CHUNK), 1)
    lower = r >= c
    for ch in range(n_chunks):
        rows = slice(ch * CHUNK, (ch + 1) * CHUNK)
        svn = _layer_norm(sv_ref[rows, :], g_ref[...], b_ref[...]).astype(_BF)
        for g in range(SGU_GROUPS):
            cols = slice(g * SGU_GROUP_DIM, (g + 1) * SGU_GROUP_DIM)
            w = jnp.where(lower, w_ref[g], 0.0).astype(_BF)
            mixed = jnp.dot(w, svn[:, cols], preferred_element_type=_F32) + bias_ref[:, cols]
            o_ref[rows, cols] = (u_ref[rows, cols] * mixed).astype(_BF)


def _sgu_prompt(sv, u, sgu_w, sgu_b, ln_g, ln_b, tm):
    m = sv.shape[0]
    bias_full = jnp.repeat(sgu_b.T, SGU_GROUP_DIM, axis=1)
    blk = pl.BlockSpec((tm, SGU_WIDTH), lambda i: (i, 0))
    full = lambda shape: pl.BlockSpec(shape, lambda i: (0,) * len(shape))
    return pl.pallas_call(
        functools.partial(_sgu_prompt_kernel, n_chunks=tm // CHUNK),
        grid=(m // tm,),
        in_specs=[blk, blk, full((SGU_GROUPS, CHUNK, CHUNK)), full((CHUNK, SGU_WIDTH)),
                  full((1, SGU_WIDTH)), full((1, SGU_WIDTH))],
        out_specs=blk,
        out_shape=jax.ShapeDtypeStruct((m, SGU_WIDTH), _BF),
        compiler_params=_cparams(1, 32),
        name="sgu_prompt",
    )(sv, u, sgu_w, bias_full, ln_g.reshape(1, -1), ln_b.reshape(1, -1))


def _sgu_sample_kernel(sv_ref, u_ref, w0_ref, b0_ref, g_ref, b_ref, svn_ref, o_ref):
    svn = _layer_norm(sv_ref[...], g_ref[...], b_ref[...])
    svn_ref[...] = svn
    w0 = w0_ref[...].astype(_BF).astype(_F32)
    mixed = w0 * svn.astype(_BF).astype(_F32) + b0_ref[...]
    o_ref[...] = (u_ref[...] * mixed).astype(_BF)


def _sgu_sample(sv, u, sgu_w, sgu_b, ln_g, ln_b):
    m = sv.shape[0]
    w0 = jnp.repeat(sgu_w[:, 0, 0], SGU_GROUP_DIM).reshape(1, SGU_WIDTH)
    b0 = jnp.repeat(sgu_b[:, 0], SGU_GROUP_DIM).reshape(1, SGU_WIDTH)
    blk = pl.BlockSpec((m, SGU_WIDTH), lambda i: (0, 0))
    vec = pl.BlockSpec((1, SGU_WIDTH), lambda i: (0, 0))
    return pl.pallas_call(
        _sgu_sample_kernel,
        grid=(1,),
        in_specs=[blk, blk, vec, vec, vec, vec],
        out_specs=(blk, blk),
        out_shape=(jax.ShapeDtypeStruct((m, SGU_WIDTH), _F32),
                   jax.ShapeDtypeStruct((m, SGU_WIDTH), _BF)),
        compiler_params=_cparams(1, 16),
        name="sgu_sample",
    )(sv, u, w0, b0, ln_g.reshape(1, -1), ln_b.reshape(1, -1))


def _merge_kernel(oa_ref, ob_ref, wa_ref, wb_ref, sga_ref, sgb_ref, m_ref):
    pa = jnp.dot(oa_ref[...], wa_ref[...], preferred_element_type=_F32)
    pb = jnp.dot(ob_ref[...], wb_ref[...], preferred_element_type=_F32)
    m_ref[...] = (sga_ref[...] * pa + sgb_ref[...] * pb).astype(_BF)


def _merge(o_a, o_b, wa_bf, wb_bf, sga, sgb, tm):
    m = o_a.shape[0]
    tn = COL_TILE
    act = pl.BlockSpec((tm, SB_WIDTH), lambda i, j: (i, 0))
    wsp = pl.BlockSpec((SB_WIDTH, tn), lambda i, j: (0, j))
    gsp = pl.BlockSpec((tm, tn), lambda i, j: (i, j))
    return pl.pallas_call(
        _merge_kernel,
        grid=(m // tm, D_MODEL // tn),
        in_specs=[act, act, wsp, wsp, gsp, gsp],
        out_specs=gsp,
        out_shape=jax.ShapeDtypeStruct((m, D_MODEL), _BF),
        compiler_params=_cparams(2, 32),
        name="gated_merge",
    )(o_a, o_b, wa_bf, wb_bf, sga, sgb)


def _out_router_kernel(m_ref, x_ref, wo_ref, g_ref, b_ref, rw_ref, rb_ref,
                       h_ref, hrows_ref, idx_ref, gate_ref, *, alpha):
    tm = m_ref.shape[0]
    y = alpha * x_ref[...] + _dot_halves(m_ref[...], wo_ref[...])
    h = _layer_norm(y, g_ref[...], b_ref[...])
    h_ref[...] = h
    for s in range(h.shape[1] // V7X_LANES):
        hrows_ref[:, s, :] = h[:, s * V7X_LANES:(s + 1) * V7X_LANES]
    logits = jnp.dot(h.astype(_BF), rw_ref[...], preferred_element_type=_F32)
    aff = _sigmoid(logits)
    sel = aff + rb_ref[...]
    e_iota = lax.broadcasted_iota(jnp.int32, (tm, N_EXPERTS), 1).astype(_F32)
    lane = lax.broadcasted_iota(jnp.int32, (tm, V7X_LANES), 1)
    idx_out = jnp.zeros((tm, V7X_LANES), _F32)
    gate_out = jnp.zeros((tm, V7X_LANES), _F32)
    gsum = jnp.zeros((tm, 1), _F32)
    for k in range(TOP_K):
        best = jnp.max(sel, axis=1, keepdims=True)
        ik = jnp.min(jnp.where(sel == best, e_iota, float(N_EXPERTS)), axis=1, keepdims=True)
        hit = e_iota == ik
        gk = jnp.sum(jnp.where(hit, aff, 0.0), axis=1, keepdims=True)
        sel = jnp.where(hit, -jnp.inf, sel)
        idx_out = jnp.where(lane == k, ik, idx_out)
        gate_out = jnp.where(lane == k, gk, gate_out)
        gsum = gsum + gk
    idx_ref[...] = idx_out.astype(jnp.int32)
    gate_ref[...] = ROUTED_SCALE * gate_out / gsum


def _out_router(merged, x, wo_bf, ln_g, ln_b, rw_bf, rbias, alpha, tm):
    m = x.shape[0]
    slabs = D_MODEL // V7X_LANES
    row = lambda w: pl.BlockSpec((tm, w), lambda i: (i, 0))
    full = lambda shape: pl.BlockSpec(shape, lambda i: (0, 0))
    return pl.pallas_call(
        functools.partial(_out_router_kernel, alpha=alpha),
        grid=(m // tm,),
        in_specs=[row(D_MODEL), row(D_MODEL), full((D_MODEL, D_MODEL)), full((1, D_MODEL)),
                  full((1, D_MODEL)), full((D_MODEL, N_EXPERTS)), full((1, N_EXPERTS))],
        out_specs=(row(D_MODEL), pl.BlockSpec((tm, slabs, V7X_LANES), lambda i: (i, 0, 0)),
                   row(V7X_LANES), row(V7X_LANES)),
        out_shape=(jax.ShapeDtypeStruct((m, D_MODEL), _F32),
                   jax.ShapeDtypeStruct((m, slabs, V7X_LANES), _F32),
                   jax.ShapeDtypeStruct((m, V7X_LANES), jnp.int32),
                   jax.ShapeDtypeStruct((m, V7X_LANES), _F32)),
        compiler_params=_cparams(1, 48),
        name="out_proj_router",
    )(merged, x, wo_bf, ln_g.reshape(1, -1), ln_b.reshape(1, -1), rw_bf,
      rbias.astype(_F32).reshape(1, -1))


def _rank_kernel(idx_ref, rank_ref, cnt_ref, carry_ref):
    t = pl.program_id(0)
    tt = idx_ref.shape[0]

    @pl.when(t == 0)
    def _():
        carry_ref[...] = jnp.zeros_like(carry_ref)

    idx = idx_ref[...]
    e_iota = lax.broadcasted_iota(jnp.int32, (tt, V7X_LANES), 1)
    hits = [e_iota == idx[:, k:k + 1] for k in range(TOP_K)]
    member = jnp.zeros((tt, V7X_LANES), _F32)
    for hit in hits:
        member = member + jnp.where(hit, 1.0, 0.0)
    r = lax.broadcasted_iota(jnp.int32, (tt, tt), 0)
    c = lax.broadcasted_iota(jnp.int32, (tt, tt), 1)
    earlier = jnp.where(c < r, 1.0, 0.0).astype(_BF)
    before = jnp.dot(earlier, member.astype(_BF), preferred_element_type=_F32) + carry_ref[0:1, :]
    rank = jnp.zeros((tt, V7X_LANES), _F32)
    for k, hit in enumerate(hits):
        rk = jnp.sum(jnp.where(hit, before, 0.0), axis=1, keepdims=True)
        rank = jnp.where(e_iota == k, rk, rank)
    rank_ref[...] = rank.astype(jnp.int32)
    carry_ref[0:1, :] = carry_ref[0:1, :] + jnp.sum(member, axis=0, keepdims=True)
    cnt_ref[...] = carry_ref[...]


def _ranks(idx_lanes):
    t = idx_lanes.shape[0]
    tt = TOK_TILE
    blk = pl.BlockSpec((tt, V7X_LANES), lambda i: (i, 0))
    return pl.pallas_call(
        _rank_kernel,
        grid=(t // tt,),
        in_specs=[blk],
        out_specs=(blk, pl.BlockSpec((8, V7X_LANES), lambda i: (0, 0))),
        out_shape=(jax.ShapeDtypeStruct((t, V7X_LANES), jnp.int32),
                   jax.ShapeDtypeStruct((8, V7X_LANES), _F32)),
        scratch_shapes=[pltpu.VMEM((8, V7X_LANES), _F32)],
        compiler_params=_cparams(1, 16),
        name="expert_ranks",
    )(idx_lanes)


def _slab_rows(r, n_slabs):
    return pl.ds(pl.multiple_of(r * SLAB_PITCH, 8), n_slabs)


def _row_copy(h_hbm, xbuf, sem, tok, r):
    return pltpu.make_async_copy(h_hbm.at[tok], xbuf.at[_slab_rows(r, h_hbm.shape[1])], sem)


def _moe_kernel(te_ref, nu_ref, tokc_ref, tokn_ref, h_hbm, wg_ref, wu_ref, wd_ref, o_ref,
                xbuf, accg, accu, act, sem, *, tile, n_kc):
    i = pl.program_id(0)
    s = pl.program_id(1)
    n_used = nu_ref[0]
    live = i < n_used

    def gather(tok_ref):
        def body(r, carry):
            _row_copy(h_hbm, xbuf, sem, tok_ref[0, 0, r], r).start()
            return carry
        lax.fori_loop(0, tile, body, 0, unroll=8)

    @pl.when((i == 0) & (s == 0))
    def _():
        gather(tokc_ref)

    @pl.when(live & (s == 0))
    def _():
        def body(r, carry):
            _row_copy(h_hbm, xbuf, sem, 0, r).wait()
            return carry
        lax.fori_loop(0, tile, body, 0, unroll=8)

    slabs = h_hbm.shape[1] // n_kc
    for c in range(n_kc):
        @pl.when(live & (s == c))
        def _(c=c):
            xk = jnp.concatenate(
                [xbuf[pl.ds(c * slabs + t, tile, stride=SLAB_PITCH), :] for t in range(slabs)],
                axis=1).astype(_BF)
            g = jnp.dot(xk, wg_ref[...].astype(_BF), preferred_element_type=_F32)
            u = jnp.dot(xk, wu_ref[...].astype(_BF), preferred_element_type=_F32)
            if c > 0:
                g = g + accg[...]
                u = u + accu[...]
            if c < n_kc - 1:
                accg[...] = g
                accu[...] = u
            else:
                act[...] = (g * _sigmoid(g) * u).astype(_BF)

    @pl.when(live & (s == n_kc) & (i + 1 < n_used))
    def _():
        gather(tokn_ref)

    @pl.when(live & (s >= n_kc))
    def _():
        res = _dot_halves(act[...], wd_ref[...].astype(_BF))
        for t in range(o_ref.shape[1]):
            o_ref[:, t, :] = res[:, t * V7X_LANES:(t + 1) * V7X_LANES]

    @pl.when(jnp.logical_not(live) & (s >= n_kc))
    def _():
        o_ref[...] = jnp.zeros_like(o_ref)


def _routed_experts(h_rows, tok_of_row, tile_expert, n_used, w_gate, w_up, w_down):
    tile = MOE_TILE
    n_tiles = tile_expert.shape[0]
    d, f = w_gate.shape[1:]
    n_kc, n_nc = d // MOE_KC, d // MOE_NC
    slabs = d // V7X_LANES
    tok3 = tok_of_row.reshape(n_tiles, 1, tile)

    def k_idx(i, s, te, nu):
        return jnp.where(i < nu[0], jnp.minimum(s, n_kc - 1), n_kc - 1)

    def n_idx(s):
        return jnp.clip(s - n_kc, 0, n_nc - 1)

    smem_row = lambda fn: pl.BlockSpec((1, 1, tile), fn, memory_space=pltpu.SMEM)
    in_specs = [
        smem_row(lambda i, s, te, nu: (i, 0, 0)),
        smem_row(lambda i, s, te, nu: (jnp.minimum(i + 1, n_tiles - 1), 0, 0)),
        pl.BlockSpec(memory_space=pl.ANY),
        pl.BlockSpec((None, MOE_KC, f), lambda i, s, te, nu: (te[i], k_idx(i, s, te, nu), 0)),
        pl.BlockSpec((None, MOE_KC, f), lambda i, s, te, nu: (te[i], k_idx(i, s, te, nu), 0)),
        pl.BlockSpec((None, f, MOE_NC), lambda i, s, te, nu: (te[i], 0, n_idx(s))),
    ]
    return pl.pallas_call(
        functools.partial(_moe_kernel, tile=tile, n_kc=n_kc),
        grid_spec=pltpu.PrefetchScalarGridSpec(
            num_scalar_prefetch=2,
            grid=(n_tiles, n_kc + n_nc),
            in_specs=in_specs,
            out_specs=pl.BlockSpec((tile, MOE_NC // V7X_LANES, V7X_LANES),
                                   lambda i, s, te, nu: (i, n_idx(s), 0)),
            scratch_shapes=[pltpu.VMEM((tile * SLAB_PITCH, V7X_LANES), _F32), pltpu.VMEM((tile, f), _F32),
                            pltpu.VMEM((tile, f), _F32), pltpu.VMEM((tile, f), _BF),
                            pltpu.SemaphoreType.DMA(())]),
        out_shape=jax.ShapeDtypeStruct((n_tiles * tile, slabs, V7X_LANES), _F32),
        compiler_params=_cparams(2, 56),
        name="routed_experts",
    )(tile_expert, n_used, tok3, tok3, h_rows, w_gate, w_up, w_down)


def _shared_kernel(x_ref, wg_ref, wu_ref, wd_ref, o_ref, accg, accu, act, *, n_kc):
    s = pl.program_id(1)
    for c in range(n_kc):
        @pl.when(s == c)
        def _(c=c):
            xk = x_ref[...].astype(_BF)
            g = jnp.dot(xk, wg_ref[...], preferred_element_type=_F32)
            u = jnp.dot(xk, wu_ref[...], preferred_element_type=_F32)
            if c > 0:
                g = g + accg[...]
                u = u + accu[...]
            if c < n_kc - 1:
                accg[...] = g
                accu[...] = u
            else:
                act[...] = (g * _sigmoid(g) * u).astype(_BF)

    @pl.when(s >= n_kc)
    def _():
        o_ref[...] = _dot_halves(act[...], wd_ref[...])


def _shared_expert(h, wg_bf, wu_bf, wd_bf, tm):
    m, d = h.shape
    f = wg_bf.shape[1]
    n_kc, n_nc = d // MOE_KC, d // SHARED_NC
    k_idx = lambda s: jnp.minimum(s, n_kc - 1)
    n_idx = lambda s: jnp.clip(s - n_kc, 0, n_nc - 1)
    return pl.pallas_call(
        functools.partial(_shared_kernel, n_kc=n_kc),
        grid=(m // tm, n_kc + n_nc),
        in_specs=[pl.BlockSpec((tm, MOE_KC), lambda i, s: (i, k_idx(s))),
                  pl.BlockSpec((MOE_KC, f), lambda i, s: (k_idx(s), 0)),
                  pl.BlockSpec((MOE_KC, f), lambda i, s: (k_idx(s), 0)),
                  pl.BlockSpec((f, SHARED_NC), lambda i, s: (0, n_idx(s)))],
        out_specs=pl.BlockSpec((tm, SHARED_NC), lambda i, s: (i, n_idx(s))),
        out_shape=jax.ShapeDtypeStruct((m, d), _F32),
        scratch_shapes=[pltpu.VMEM((tm, f), _F32), pltpu.VMEM((tm, f), _F32),
                        pltpu.VMEM((tm, f), _BF)],
        compiler_params=_cparams(2, 40),
        name="shared_expert",
    )(h, wg_bf, wu_bf, wd_bf)


def _combine_kernel(destc_ref, destn_ref, gate_ref, h_ref, sh_ref, outs_hbm, g_ref, b_ref,
                    yp_ref, ys_ref, gbuf, pre_ref, sem, *, alpha, n_prompt_tiles):
    i = pl.program_id(0)
    n_steps = pl.num_programs(0)
    tt = h_ref.shape[0]
    slabs = outs_hbm.shape[1]
    slot = lax.rem(i, 2)

    def region(sl, k):
        base = (sl * TOP_K + k) * (tt * SLAB_PITCH)
        return base if isinstance(base, int) else pl.multiple_of(base, 8)

    def copy(src_row, sl, k, r):
        rows = pl.ds(region(sl, k) + pl.multiple_of(r * SLAB_PITCH, 8), slabs)
        return pltpu.make_async_copy(outs_hbm.at[src_row], gbuf.at[rows], sem.at[sl])

    def issue(dest_ref, sl):
        def body(r, carry):
            for k in range(TOP_K):
                copy(dest_ref[0, 0, r * TOP_K + k], sl, k, r).start()
            return carry
        lax.fori_loop(0, tt, body, 0, unroll=2)

    @pl.when(i == 0)
    def _():
        issue(destc_ref, 0)

    @pl.when(i + 1 < n_steps)
    def _():
        issue(destn_ref, 1 - slot)

    def drain(r, carry):
        for k in range(TOP_K):
            copy(0, slot, k, r).wait()
        return carry

    lax.fori_loop(0, tt, drain, 0, unroll=2)
    gate = gate_ref[...]
    gate_b = [jnp.broadcast_to(gate[:, k:k + 1], (tt, V7X_LANES)) for k in range(TOP_K)]
    for s in range(slabs):
        cols = slice(s * V7X_LANES, (s + 1) * V7X_LANES)
        acc = alpha * h_ref[:, cols] + sh_ref[:, cols]
        for k in range(TOP_K):
            acc = acc + gate_b[k] * gbuf[pl.ds(region(slot, k) + s, tt, stride=SLAB_PITCH), :]
        pre_ref[:, cols] = acc
    y = _layer_norm(pre_ref[...], g_ref[...], b_ref[...])

    @pl.when(i < n_prompt_tiles)
    def _():
        yp_ref[...] = y

    @pl.when(i >= n_prompt_tiles)
    def _():
        ys_ref[...] = y


def _combine(dest, gate_lanes, h_all, shared_all, outs, ln_g, ln_b, alpha, n_prompt, n_sample):
    tt = TOK_TILE
    t_all, d = h_all.shape
    n_pt, n_st = n_prompt // tt, n_sample // tt
    n_steps = n_pt + n_st
    dest3 = dest.reshape(n_steps, 1, tt * TOP_K)
    row = lambda w: pl.BlockSpec((tt, w), lambda i: (i, 0))
    vec = pl.BlockSpec((1, d), lambda i: (0, 0))
    dest_spec = lambda fn: pl.BlockSpec((1, 1, tt * TOP_K), fn, memory_space=pltpu.SMEM)
    return pl.pallas_call(
        functools.partial(_combine_kernel, alpha=alpha, n_prompt_tiles=n_pt),
        grid=(n_steps,),
        in_specs=[dest_spec(lambda i: (i, 0, 0)),
                  dest_spec(lambda i: (jnp.minimum(i + 1, n_steps - 1), 0, 0)),
                  row(V7X_LANES), row(d), row(d), pl.BlockSpec(memory_space=pl.ANY), vec, vec],
        out_specs=(pl.BlockSpec((tt, d), lambda i: (jnp.minimum(i, n_pt - 1), 0)),
                   pl.BlockSpec((tt, d), lambda i: (jnp.clip(i - n_pt, 0, n_st - 1), 0))),
        out_shape=(jax.ShapeDtypeStruct((n_prompt, d), _F32),
                   jax.ShapeDtypeStruct((n_sample, d), _F32)),
        scratch_shapes=[pltpu.VMEM((2 * TOP_K * tt * SLAB_PITCH, V7X_LANES), _F32),
                        pltpu.VMEM((tt, d), _F32), pltpu.SemaphoreType.DMA((2,))],
        compiler_params=_cparams(1, 48),
        name="moe_combine",
    )(dest3, dest3, gate_lanes, h_all, shared_all, outs, ln_g.reshape(1, -1), ln_b.reshape(1, -1))


def _routing_tables(idx_lanes, rank_lanes, counts_row):
    n_tok = idx_lanes.shape[0]
    n_assign = n_tok * TOP_K
    n_tiles = -(-n_assign // MOE_TILE) + N_EXPERTS
    idx = idx_lanes[:, :TOP_K]
    counts = counts_row[0, :N_EXPERTS].astype(jnp.int32)
    padded = (counts + MOE_TILE - 1) // MOE_TILE * MOE_TILE
    pad_end = jnp.cumsum(padded)
    pad_start = pad_end - padded
    dest = pad_start[idx] + rank_lanes[:, :TOP_K]
    n_used = pad_end[-1] // MOE_TILE
    tiles = jnp.minimum(jnp.arange(n_tiles, dtype=jnp.int32), n_used - 1)
    ends_before = (pad_end[None, :] <= (tiles * MOE_TILE)[:, None]).astype(jnp.int32)
    tile_expert = jnp.minimum(jnp.sum(ends_before, axis=1), N_EXPERTS - 1)
    tok_of_row = jnp.zeros((n_tiles * MOE_TILE,), jnp.int32).at[dest.reshape(-1)].set(
        jnp.arange(n_assign, dtype=jnp.int32) // TOP_K, unique_indices=True)
    return (dest.astype(jnp.int32), tok_of_row, tile_expert.astype(jnp.int32),
            n_used.astype(jnp.int32).reshape(1))


def _mixers_to_router(x2, o_a, o_b, sga, sgb, wts, alpha, tm):
    merged = _merge(o_a, o_b, wts["w_proj_a"], wts["w_proj_b"], sga, sgb, tm)
    return _out_router(merged, x2, wts["w_out"], wts["ln1_g"], wts["ln1_b"],
                       wts["router_w"], wts["router_bias"], alpha, min(tm, ROUTER_TILE))


def _layer(yp, ys, cache_k, cache_v, page_table, wts, alpha):
    b, s, d = yp.shape
    bd, t, _ = ys.shape
    n_p, n_s = b * s, bd * t
    xp, xs = yp.reshape(n_p, d), ys.reshape(n_s, d)

    q, k, kb, v, vb, u, sv, sga, sgb = _in_projection(xp, wts["w_in"], ROW_TILE)
    o_a = _sb_prompt(q, kb, vb, wts["sb_bias"], b, s)
    o_b = _sgu_prompt(sv, u, wts["sgu_w"], wts["sgu_b"], wts["sgu_ln_g"], wts["sgu_ln_b"], ROW_TILE)
    h_p, hr_p, idx_p, gate_p = _mixers_to_router(xp, o_a, o_b, sga, sgb, wts, alpha, ROW_TILE)
    k_p = k.reshape(b, s, SB_HEADS, SB_HEAD_DIM)
    v_p = v.reshape(b, s, SB_HEADS, SB_HEAD_DIM)

    q, k, kb, v, vb, u, sv, sga, sgb = _in_projection(xs, wts["w_in"], n_s)
    o_a = _sb_sample(q.astype(_F32), k, v, cache_k, cache_v, page_table, wts["sb_bias"])
    svn, o_b = _sgu_sample(sv, u, wts["sgu_w"], wts["sgu_b"], wts["sgu_ln_g"], wts["sgu_ln_b"])
    h_s, hr_s, idx_s, gate_s = _mixers_to_router(xs, o_a.astype(_BF), o_b, sga, sgb, wts, alpha, n_s)
    k_s = k.reshape(bd, t, SB_HEADS, SB_HEAD_DIM)
    v_s = v.reshape(bd, t, SB_HEADS, SB_HEAD_DIM)

    h_all = jnp.concatenate([h_p, h_s], axis=0)
    h_rows = jnp.concatenate([hr_p, hr_s], axis=0)
    idx_all = jnp.concatenate([idx_p, idx_s], axis=0)
    gate_all = jnp.concatenate([gate_p, gate_s], axis=0)
    rank, counts = _ranks(idx_all)
    dest, tok_of_row, tile_expert, n_used = _routing_tables(idx_all, rank, counts)
    outs = _routed_experts(h_rows, tok_of_row, tile_expert, n_used,
                           wts["exp_w_gate"], wts["exp_w_up"], wts["exp_w_down"])
    sh_p = _shared_expert(h_p, wts["sh_w_gate"], wts["sh_w_up"], wts["sh_w_down"], ROW_TILE)
    sh_s = _shared_expert(h_s, wts["sh_w_gate"], wts["sh_w_up"], wts["sh_w_down"], n_s)
    shared_all = jnp.concatenate([sh_p, sh_s], axis=0)
    yp2, ys2 = _combine(dest, gate_all, h_all, shared_all, outs, wts["ln2_g"], wts["ln2_b"],
                        alpha, n_p, n_s)
    return (yp2.reshape(b, s, d), ys2.reshape(bd, t, d), k_p, v_p, k_s, v_s,
            svn.reshape(bd, t, SGU_WIDTH))


def kernel(x_prompt, x_sample, cache_k, cache_v, page_table, w_in, sb_bias, sgu_ln_g, sgu_ln_b, sgu_w, sgu_b, w_proj_a, w_proj_b, w_out, ln1_g, ln1_b, router_w, router_bias, exp_w_gate, exp_w_up, exp_w_down, sh_w_gate, sh_w_up, sh_w_down, ln2_g, ln2_b):
    depth = w_in.shape[0]
    alpha = (2.0 * depth) ** 0.25
    yp, ys = x_prompt, x_sample
    kp, vp, ksm, vsm, svs = [], [], [], [], []
    for l in range(depth):
        wts = {
            "w_in": w_in[l].astype(_BF), "sb_bias": sb_bias[l],
            "sgu_ln_g": sgu_ln_g[l], "sgu_ln_b": sgu_ln_b[l], "sgu_w": sgu_w[l], "sgu_b": sgu_b[l],
            "w_proj_a": w_proj_a[l].astype(_BF), "w_proj_b": w_proj_b[l].astype(_BF),
            "w_out": w_out[l].astype(_BF), "ln1_g": ln1_g[l], "ln1_b": ln1_b[l],
            "router_w": router_w[l].astype(_BF), "router_bias": router_bias[l],
            "exp_w_gate": exp_w_gate[l], "exp_w_up": exp_w_up[l], "exp_w_down": exp_w_down[l],
            "sh_w_gate": sh_w_gate[l].astype(_BF), "sh_w_up": sh_w_up[l].astype(_BF),
            "sh_w_down": sh_w_down[l].astype(_BF), "ln2_g": ln2_g[l], "ln2_b": ln2_b[l],
        }
        yp, ys, k_p, v_p, k_s, v_s, svn = _layer(yp, ys, cache_k[l], cache_v[l], page_table,
                                                 wts, alpha)
        kp.append(k_p)
        vp.append(v_p)
        ksm.append(k_s)
        vsm.append(v_s)
        svs.append(svn)
    return (yp, ys, jnp.stack(kp), jnp.stack(vp), jnp.stack(ksm), jnp.stack(vsm), jnp.stack(svs))
```

```python
import functools

import jax
import jax.numpy as jnp
from jax import lax
from jax.experimental import pallas as pl
from jax.experimental.pallas import tpu as pltpu

D_MODEL = 2048
SB_HEADS = 16
SB_HEAD_DIM = 64
SB_WIDTH = SB_HEADS * SB_HEAD_DIM
SGU_GROUPS = 8
SGU_GROUP_DIM = 128
SGU_WIDTH = SGU_GROUPS * SGU_GROUP_DIM
CHUNK = 128
N_EXPERTS = 64
TOP_K = 6
D_EXPERT = 1408
ROUTED_SCALE = 2.5
LN_EPS = 1e-5
SB_SCALE = SB_HEAD_DIM ** -0.5

V7X_LANES = 128
V7X_VMEM_BYTES = 64 * 2 ** 20

ROW_TILE = 512
COL_TILE = 512
ATT_TQ = 256
ATT_KG = 512
ATT_PAGES = 4
MOE_TILE = 512
MOE_KC = 512
MOE_NC = 1024
SHARED_NC = 512
TOK_TILE = 128
ROUTER_TILE = 256
SLAB_PITCH = 24

_BF = jnp.bfloat16
_F32 = jnp.float32


def _cparams(n_axes, vmem_mib):
    return pltpu.CompilerParams(
        dimension_semantics=("arbitrary",) * n_axes,
        vmem_limit_bytes=min(vmem_mib * 2 ** 20, V7X_VMEM_BYTES - 6 * 2 ** 20))


def _layer_norm(x, g, b):
    mu = jnp.mean(x, axis=-1, keepdims=True)
    xc = x - mu
    var = jnp.mean(xc * xc, axis=-1, keepdims=True)
    return xc * lax.rsqrt(var + LN_EPS) * g + b


def _gelu(x):
    return 0.5 * x * (1.0 + lax.erf(x * 0.7071067811865476))


def _sigmoid(x):
    return 1.0 / (1.0 + jnp.exp(-x))


def _softplus(z):
    return jnp.maximum(z, 0.0) + jnp.log(1.0 + jnp.exp(-jnp.abs(z)))


def _dot_halves(x, w):
    half = w.shape[1] // 2
    return jnp.concatenate([jnp.dot(x, w[:, :half], preferred_element_type=_F32),
                            jnp.dot(x, w[:, half:], preferred_element_type=_F32)], axis=1)


def _split_dot(x, u):
    hi = x.astype(_BF)
    lo = (x - hi.astype(_F32)).astype(_BF)
    return (jnp.dot(hi, u, preferred_element_type=_F32)
            + jnp.dot(lo, u, preferred_element_type=_F32))


def _in_sections(tn):
    widths = (SB_WIDTH, SB_WIDTH, SB_WIDTH, SGU_WIDTH, SGU_WIDTH, D_MODEL, D_MODEL)
    starts, s = [], 0
    for w in widths:
        starts.append(s // tn)
        s += w
    return starts, [w // tn for w in widths]


def _in_proj_kernel(x_ref, w_ref, q_ref, k_ref, kb_ref, v_ref, vb_ref, u_ref, sv_ref,
                    ga_ref, gb_ref, xb_ref, *, starts, counts):
    j = pl.program_id(1)

    @pl.when(j == 0)
    def _():
        xb_ref[...] = x_ref[...].astype(_BF)

    acc = _dot_halves(xb_ref[...], w_ref[...])

    def section(n):
        return (j >= starts[n]) & (j < starts[n] + counts[n])

    @pl.when(section(0))
    def _():
        q_ref[...] = (acc * SB_SCALE).astype(_BF)

    @pl.when(section(1))
    def _():
        k_ref[...] = acc
        kb_ref[...] = acc.astype(_BF)

    @pl.when(section(2))
    def _():
        v_ref[...] = acc
        vb_ref[...] = acc.astype(_BF)

    @pl.when(section(3))
    def _():
        u_ref[...] = _gelu(acc)

    @pl.when(section(4))
    def _():
        sv_ref[...] = _gelu(acc)

    @pl.when(section(5))
    def _():
        ga_ref[...] = _sigmoid(acc)

    @pl.when(section(6))
    def _():
        gb_ref[...] = _sigmoid(acc)


def _in_projection(x, w_bf, tm):
    m, d = x.shape
    tn = COL_TILE
    starts, counts = _in_sections(tn)
    n_tiles = w_bf.shape[1] // tn

    def out_spec(n):
        return pl.BlockSpec(
            (tm, tn), lambda i, j, n=n: (i, jnp.clip(j - starts[n], 0, counts[n] - 1)))

    shp = lambda w, dt: jax.ShapeDtypeStruct((m, w), dt)
    out_shape = (shp(SB_WIDTH, _BF), shp(SB_WIDTH, _F32), shp(SB_WIDTH, _BF),
                 shp(SB_WIDTH, _F32), shp(SB_WIDTH, _BF), shp(SGU_WIDTH, _F32),
                 shp(SGU_WIDTH, _F32), shp(D_MODEL, _F32), shp(D_MODEL, _F32))
    out_specs = (out_spec(0), out_spec(1), out_spec(1), out_spec(2), out_spec(2),
                 out_spec(3), out_spec(4), out_spec(5), out_spec(6))
    return pl.pallas_call(
        functools.partial(_in_proj_kernel, starts=starts, counts=counts),
        grid=(m // tm, n_tiles),
        in_specs=[pl.BlockSpec((tm, d), lambda i, j: (i, 0)),
                  pl.BlockSpec((d, tn), lambda i, j: (0, j))],
        out_specs=out_specs,
        out_shape=out_shape,
        scratch_shapes=[pltpu.VMEM((tm, d), _BF)],
        compiler_params=_cparams(2, 40),
        name="in_projection",
    )(x, w_bf)


def _sb_prompt_kernel(bias_ref, q_ref, k_ref, v_ref, o_ref, *, tq, kg):
    p = pl.program_id(1)
    i = pl.program_id(2)
    nb = kg // V7X_LANES
    q2 = q_ref[...].astype(_F32)
    lane = lax.broadcasted_iota(jnp.int32, (tq, V7X_LANES), 1)
    row_pos = i * tq + lax.broadcasted_iota(jnp.int32, (tq, kg), 0)
    col = lax.broadcasted_iota(jnp.int32, (tq, kg), 1)
    ur = lax.broadcasted_iota(jnp.int32, (2 * V7X_LANES, 2 * V7X_LANES), 0)
    uc = lax.broadcasted_iota(jnp.int32, (2 * V7X_LANES, 2 * V7X_LANES), 1)
    later_in_block = jnp.where(((ur >= V7X_LANES) == (uc >= V7X_LANES)) & (ur > uc),
                               1.0, 0.0).astype(_BF)
    n_groups = ((i + 1) * tq + kg - 1) // kg
    in_head = [(lane >= hh * SB_HEAD_DIM) & (lane < (hh + 1) * SB_HEAD_DIM) for hh in range(2)]
    qm = [jnp.where(in_head[hh], q2, 0.0).astype(_BF) for hh in range(2)]
    bias = [bias_ref[2 * p + hh] for hh in range(2)]

    def body(t, carry):
        off = pl.multiple_of((n_groups - 1 - t) * kg, kg)
        kblk = k_ref[pl.ds(off, kg), :]
        vblk = v_ref[pl.ds(off, kg), :]
        causal = (off + col) < row_pos
        new = []
        for hh in range(2):
            run, acc = carry[2 * hh], carry[2 * hh + 1]
            z = lax.dot_general(qm[hh], kblk, (((1,), (1,)), ((), ())),
                                preferred_element_type=_F32) + bias[hh]
            sp = _softplus(z)
            spm = jnp.where(causal, sp, 0.0)
            spb = spm.astype(_BF)
            pairs = jnp.concatenate(
                [spb[:, g * 2 * V7X_LANES:(g + 1) * 2 * V7X_LANES] for g in range(nb // 2)], axis=0)
            within = jnp.dot(pairs, later_in_block, preferred_element_type=_F32)
            a_blocks = [None] * nb
            for c in reversed(range(nb)):
                cols = slice(c * V7X_LANES, (c + 1) * V7X_LANES)
                w_c = within[(c // 2) * tq:(c // 2 + 1) * tq,
                             (c % 2) * V7X_LANES:(c % 2 + 1) * V7X_LANES]
                log_a = z[:, cols] - sp[:, cols] - w_c - run
                a_blocks[c] = jnp.where(causal[:, cols], jnp.exp(log_a), 0.0).astype(_BF)
                run = run + jnp.sum(spm[:, cols], axis=1, keepdims=True)
            a = jnp.concatenate(a_blocks, axis=1)
            acc = acc + jnp.dot(a, vblk, preferred_element_type=_F32)
            new += [run, acc]
        return tuple(new)

    zero = jnp.zeros((tq, V7X_LANES), _F32)
    zero_col = jnp.zeros((tq, 1), _F32)
    res = lax.fori_loop(0, n_groups, body, (zero_col, zero, zero_col, zero))
    o_ref[...] = jnp.where(in_head[0], res[1], res[3]).astype(_BF)


def _sb_prompt(q_bf, k_bf, v_bf, sb_bias, b, s):
    tq, kg = min(ATT_TQ, s), min(ATT_KG, s)
    q3, k3, v3 = (a.reshape(b, s, SB_WIDTH) for a in (q_bf, k_bf, v_bf))
    kv_spec = pl.BlockSpec((None, s, V7X_LANES), lambda bi, p, i, bias: (bi, 0, p))
    q_spec = pl.BlockSpec((None, tq, V7X_LANES), lambda bi, p, i, bias: (bi, i, p))
    out = pl.pallas_call(
        functools.partial(_sb_prompt_kernel, tq=tq, kg=kg),
        grid_spec=pltpu.PrefetchScalarGridSpec(
            num_scalar_prefetch=1,
            grid=(b, SB_WIDTH // V7X_LANES, s // tq),
            in_specs=[q_spec, kv_spec, kv_spec],
            out_specs=q_spec),
        out_shape=jax.ShapeDtypeStruct((b, s, SB_WIDTH), _BF),
        compiler_params=_cparams(3, 32),
        name="sb_attention_prompt",
    )(sb_bias.astype(_F32), q3, k3, v3)
    return out.reshape(b * s, SB_WIDTH)


def _sb_sample_kernel(pt_ref, q_ref, kn_ref, vn_ref, bias_ref, *refs, page, pps, past_len):
    kc_refs, vc_refs = refs[:pps], refs[pps:2 * pps]
    o_ref, acc_ref, run_ref = refs[2 * pps:]
    j = pl.program_id(1)
    n_steps = pl.num_programs(1)
    head = lax.broadcasted_iota(jnp.int32, (SB_HEADS, SB_WIDTH), 0)
    lane = lax.broadcasted_iota(jnp.int32, (SB_HEADS, SB_WIDTH), 1)
    in_head = (lane >= head * SB_HEAD_DIM) & (lane < (head + 1) * SB_HEAD_DIM)
    q_bd = jnp.where(in_head, q_ref[...], 0.0).astype(_BF)
    bias = bias_ref[...]
    nt = (((1,), (1,)), ((), ()))

    @pl.when(j == 0)
    def _():
        k_new = kn_ref[...].astype(_BF).astype(_F32)
        v_new = vn_ref[...].astype(_BF).astype(_F32)
        z_new = jnp.sum(q_bd.astype(_F32) * k_new, axis=1, keepdims=True) + bias
        k_pos = past_len + lax.broadcasted_iota(jnp.int32, (SB_HEADS, 1), 1)
        q_pos = past_len + lax.broadcasted_iota(jnp.int32, (SB_HEADS, 1), 1)
        causal = k_pos < q_pos
        sp_new = _softplus(z_new)
        a_new = jnp.where(causal, jnp.exp(z_new - sp_new), 0.0).astype(_BF).astype(_F32)
        acc_ref[...] = a_new * v_new
        run_ref[...] = jnp.where(causal, sp_new, 0.0)

    ur = lax.broadcasted_iota(jnp.int32, (page, page), 0)
    uc = lax.broadcasted_iota(jnp.int32, (page, page), 1)
    upper = jnp.where(ur > uc, 1.0, 0.0).astype(_BF)
    zs = [jnp.dot(q_bd, kc[...].astype(_BF), preferred_element_type=_F32) + bias
          for kc in kc_refs]
    run = run_ref[...]
    acc = acc_ref[...]
    for m in reversed(range(pps)):
        sp = _softplus(zs[m])
        within = _split_dot(sp, upper)
        a = jnp.exp(zs[m] - sp - within - run).astype(_BF)
        acc = acc + lax.dot_general(a, vc_refs[m][...].astype(_BF), nt,
                                    preferred_element_type=_F32)
        run = run + jnp.sum(sp, axis=1, keepdims=True)
    acc_ref[...] = acc
    run_ref[...] = run

    @pl.when(j == n_steps - 1)
    def _():
        o_ref[...] = jnp.sum(jnp.where(in_head, acc, 0.0), axis=0, keepdims=True)


def _sb_sample(q, k_new, v_new, cache_k, cache_v, page_table, sb_bias):
    bd = q.shape[0]
    n_pages = page_table.shape[1]
    n_pool, page = cache_k.shape[:2]
    pps = max(p for p in range(1, ATT_PAGES + 1) if n_pages % p == 0)
    ck = jnp.transpose(cache_k, (0, 2, 3, 1)).reshape(n_pool, SB_WIDTH, page)
    cv = jnp.transpose(cache_v, (0, 2, 3, 1)).reshape(n_pool, SB_WIDTH, page)
    row = pl.BlockSpec((None, 1, SB_WIDTH), lambda b, j, pt: (b, 0, 0))

    def page_spec(m):
        return pl.BlockSpec(
            (None, SB_WIDTH, page),
            lambda b, j, pt, m=m: (pt[b * n_pages + n_pages - (j + 1) * pps + m], 0, 0))

    r3 = lambda a: a.reshape(bd, 1, SB_WIDTH)
    out = pl.pallas_call(
        functools.partial(_sb_sample_kernel, page=page, pps=pps, past_len=n_pages * page),
        grid_spec=pltpu.PrefetchScalarGridSpec(
            num_scalar_prefetch=1,
            grid=(bd, n_pages // pps),
            in_specs=([row, row, row, pl.BlockSpec((SB_HEADS, 1), lambda b, j, pt: (0, 0))]
                      + [page_spec(m) for m in range(pps)] * 2),
            out_specs=row,
            scratch_shapes=[pltpu.VMEM((SB_HEADS, SB_WIDTH), _F32),
                            pltpu.VMEM((SB_HEADS, 1), _F32)]),
        out_shape=jax.ShapeDtypeStruct((bd, 1, SB_WIDTH), _F32),
        compiler_params=_cparams(2, 32),
        name="sb_attention_sample",
    )(page_table.reshape(-1).astype(jnp.int32), r3(q), r3(k_new), r3(v_new),
      sb_bias.astype(_F32).reshape(SB_HEADS, 1), *([ck] * pps), *([cv] * pps))
    return out.reshape(bd, SB_WIDTH)


def _sgu_prompt_kernel(sv_ref, u_ref, w_ref, bias_ref, g_ref, b_ref, o_ref, *, n_chunks):
    r = lax.broadcasted_iota(jnp.int32, (CHUNK, CHUNK), 0)
    c = lax.broadcasted_iota(jnp.int32, (CHUNK, CHUNK), 1)
    lower = r >= c
    for ch in range(n_chunks):
        rows = slice(ch * CHUNK, (ch + 1) * CHUNK)
        svn = _layer_norm(sv_ref[rows, :], g_ref[...], b_ref[...]).astype(_BF)
        for g in range(SGU_GROUPS):
            cols = slice(g * SGU_GROUP_DIM, (g + 1) * SGU_GROUP_DIM)
            w = jnp.where(lower, w_ref[g], 0.0).astype(_BF)
            mixed = jnp.dot(w, svn[:, cols], preferred_element_type=_F32) + bias_ref[:, cols]
            o_ref[rows, cols] = (u_ref[rows, cols] * mixed).astype(_BF)


def _sgu_prompt(sv, u, sgu_w, sgu_b, ln_g, ln_b, tm):
    m = sv.shape[0]
    bias_full = jnp.repeat(sgu_b.T, SGU_GROUP_DIM, axis=1)
    blk = pl.BlockSpec((tm, SGU_WIDTH), lambda i: (i, 0))
    full = lambda shape: pl.BlockSpec(shape, lambda i: (0,) * len(shape))
    return pl.pallas_call(
        functools.partial(_sgu_prompt_kernel, n_chunks=tm // CHUNK),
        grid=(m // tm,),
        in_specs=[blk, blk, full((SGU_GROUPS, CHUNK, CHUNK)), full((CHUNK, SGU_WIDTH)),
                  full((1, SGU_WIDTH)), full((1, SGU_WIDTH))],
        out_specs=blk,
        out_shape=jax.ShapeDtypeStruct((m, SGU_WIDTH), _BF),
        compiler_params=_cparams(1, 32),
        name="sgu_prompt",
    )(sv, u, sgu_w, bias_full, ln_g.reshape(1, -1), ln_b.reshape(1, -1))


def _sgu_sample_kernel(sv_ref, u_ref, w0_ref, b0_ref, g_ref, b_ref, svn_ref, o_ref):
    svn = _layer_norm(sv_ref[...], g_ref[...], b_ref[...])
    svn_ref[...] = svn
    w0 = w0_ref[...].astype(_BF).astype(_F32)
    mixed = w0 * svn.astype(_BF).astype(_F32) + b0_ref[...]
    o_ref[...] = (u_ref[...] * mixed).astype(_BF)


def _sgu_sample(sv, u, sgu_w, sgu_b, ln_g, ln_b):
    m = sv.shape[0]
    w0 = jnp.repeat(sgu_w[:, 0, 0], SGU_GROUP_DIM).reshape(1, SGU_WIDTH)
    b0 = jnp.repeat(sgu_b[:, 0], SGU_GROUP_DIM).reshape(1, SGU_WIDTH)
    blk = pl.BlockSpec((m, SGU_WIDTH), lambda i: (0, 0))
    vec = pl.BlockSpec((1, SGU_WIDTH), lambda i: (0, 0))
    return pl.pallas_call(
        _sgu_sample_kernel,
        grid=(1,),
        in_specs=[blk, blk, vec, vec, vec, vec],
        out_specs=(blk, blk),
        out_shape=(jax.ShapeDtypeStruct((m, SGU_WIDTH), _F32),
                   jax.ShapeDtypeStruct((m, SGU_WIDTH), _BF)),
        compiler_params=_cparams(1, 16),
        name="sgu_sample",
    )(sv, u, w0, b0, ln_g.reshape(1, -1), ln_b.reshape(1, -1))


def _merge_kernel(oa_ref, ob_ref, wa_ref, wb_ref, sga_ref, sgb_ref, m_ref):
    pa = jnp.dot(oa_ref[...], wa_ref[...], preferred_element_type=_F32)
    pb = jnp.dot(ob_ref[...], wb_ref[...], preferred_element_type=_F32)
    m_ref[...] = (sga_ref[...] * pa + sgb_ref[...] * pb).astype(_BF)


def _merge(o_a, o_b, wa_bf, wb_bf, sga, sgb, tm):
    m = o_a.shape[0]
    tn = COL_TILE
    act = pl.BlockSpec((tm, SB_WIDTH), lambda i, j: (i, 0))
    wsp = pl.BlockSpec((SB_WIDTH, tn), lambda i, j: (0, j))
    gsp = pl.BlockSpec((tm, tn), lambda i, j: (i, j))
    return pl.pallas_call(
        _merge_kernel,
        grid=(m // tm, D_MODEL // tn),
        in_specs=[act, act, wsp, wsp, gsp, gsp],
        out_specs=gsp,
        out_shape=jax.ShapeDtypeStruct((m, D_MODEL), _BF),
        compiler_params=_cparams(2, 32),
        name="gated_merge",
    )(o_a, o_b, wa_bf, wb_bf, sga, sgb)


def _out_router_kernel(m_ref, x_ref, wo_ref, g_ref, b_ref, rw_ref, rb_ref,
                       h_ref, hrows_ref, idx_ref, gate_ref, *, alpha):
    tm = m_ref.shape[0]
    y = alpha * x_ref[...] + _dot_halves(m_ref[...], wo_ref[...])
    h = _layer_norm(y, g_ref[...], b_ref[...])
    h_ref[...] = h
    for s in range(h.shape[1] // V7X_LANES):
        hrows_ref[:, s, :] = h[:, s * V7X_LANES:(s + 1) * V7X_LANES]
    logits = jnp.dot(h.astype(_BF), rw_ref[...], preferred_element_type=_F32)
    aff = _sigmoid(logits)
    sel = aff + rb_ref[...]
    e_iota = lax.broadcasted_iota(jnp.int32, (tm, N_EXPERTS), 1).astype(_F32)
    lane = lax.broadcasted_iota(jnp.int32, (tm, V7X_LANES), 1)
    idx_out = jnp.zeros((tm, V7X_LANES), _F32)
    gate_out = jnp.zeros((tm, V7X_LANES), _F32)
    gsum = jnp.zeros((tm, 1), _F32)
    for k in range(TOP_K):
        best = jnp.max(sel, axis=1, keepdims=True)
        ik = jnp.min(jnp.where(sel == best, e_iota, float(N_EXPERTS)), axis=1, keepdims=True)
        hit = e_iota == ik
        gk = jnp.sum(jnp.where(hit, aff, 0.0), axis=1, keepdims=True)
        sel = jnp.where(hit, -jnp.inf, sel)
        idx_out = jnp.where(lane == k, ik, idx_out)
        gate_out = jnp.where(lane == k, gk, gate_out)
        gsum = gsum + gk
    idx_ref[...] = idx_out.astype(jnp.int32)
    gate_ref[...] = ROUTED_SCALE * gate_out / gsum


def _out_router(merged, x, wo_bf, ln_g, ln_b, rw_bf, rbias, alpha, tm):
    m = x.shape[0]
    slabs = D_MODEL // V7X_LANES
    row = lambda w: pl.BlockSpec((tm, w), lambda i: (i, 0))
    full = lambda shape: pl.BlockSpec(shape, lambda i: (0, 0))
    return pl.pallas_call(
        functools.partial(_out_router_kernel, alpha=alpha),
        grid=(m // tm,),
        in_specs=[row(D_MODEL), row(D_MODEL), full((D_MODEL, D_MODEL)), full((1, D_MODEL)),
                  full((1, D_MODEL)), full((D_MODEL, N_EXPERTS)), full((1, N_EXPERTS))],
        out_specs=(row(D_MODEL), pl.BlockSpec((tm, slabs, V7X_LANES), lambda i: (i, 0, 0)),
                   row(V7X_LANES), row(V7X_LANES)),
        out_shape=(jax.ShapeDtypeStruct((m, D_MODEL), _F32),
                   jax.ShapeDtypeStruct((m, slabs, V7X_LANES), _F32),
                   jax.ShapeDtypeStruct((m, V7X_LANES), jnp.int32),
                   jax.ShapeDtypeStruct((m, V7X_LANES), _F32)),
        compiler_params=_cparams(1, 48),
        name="out_proj_router",
    )(merged, x, wo_bf, ln_g.reshape(1, -1), ln_b.reshape(1, -1), rw_bf,
      rbias.astype(_F32).reshape(1, -1))


def _rank_kernel(idx_ref, rank_ref, cnt_ref, carry_ref):
    t = pl.program_id(0)
    tt = idx_ref.shape[0]

    @pl.when(t == 0)
    def _():
        carry_ref[...] = jnp.zeros_like(carry_ref)

    idx = idx_ref[...]
    e_iota = lax.broadcasted_iota(jnp.int32, (tt, V7X_LANES), 1)
    hits = [e_iota == idx[:, k:k + 1] for k in range(TOP_K)]
    member = jnp.zeros((tt, V7X_LANES), _F32)
    for hit in hits:
        member = member + jnp.where(hit, 1.0, 0.0)
    r = lax.broadcasted_iota(jnp.int32, (tt, tt), 0)
    c = lax.broadcasted_iota(jnp.int32, (tt, tt), 1)
    earlier = jnp.where(c < r, 1.0, 0.0).astype(_BF)
    before = jnp.dot(earlier, member.astype(_BF), preferred_element_type=_F32) + carry_ref[0:1, :]
    rank = jnp.zeros((tt, V7X_LANES), _F32)
    for k, hit in enumerate(hits):
        rk = jnp.sum(jnp.where(hit, before, 0.0), axis=1, keepdims=True)
        rank = jnp.where(e_iota == k, rk, rank)
    rank_ref[...] = rank.astype(jnp.int32)
    carry_ref[0:1, :] = carry_ref[0:1, :] + jnp.sum(member, axis=0, keepdims=True)
    cnt_ref[...] = carry_ref[...]


def _ranks(idx_lanes):
    t = idx_lanes.shape[0]
    tt = TOK_TILE
    blk = pl.BlockSpec((tt, V7X_LANES), lambda i: (i, 0))
    return pl.pallas_call(
        _rank_kernel,
        grid=(t // tt,),
        in_specs=[blk],
        out_specs=(blk, pl.BlockSpec((8, V7X_LANES), lambda i: (0, 0))),
        out_shape=(jax.ShapeDtypeStruct((t, V7X_LANES), jnp.int32),
                   jax.ShapeDtypeStruct((8, V7X_LANES), _F32)),
        scratch_shapes=[pltpu.VMEM((8, V7X_LANES), _F32)],
        compiler_params=_cparams(1, 16),
        name="expert_ranks",
    )(idx_lanes)


def _slab_rows(r, n_slabs):
    return pl.ds(pl.multiple_of(r * SLAB_PITCH, 8), n_slabs)


def _row_copy(h_hbm, xbuf, sem, tok, r):
    return pltpu.make_async_copy(h_hbm.at[tok], xbuf.at[_slab_rows(r, h_hbm.shape[1])], sem)


def _moe_kernel(te_ref, nu_ref, tokc_ref, tokn_ref, h_hbm, wg_ref, wu_ref, wd_ref, o_ref,
                xbuf, accg, accu, act, sem, *, tile, n_kc):
    i = pl.program_id(0)
    s = pl.program_id(1)
    n_used = nu_ref[0]
    live = i < n_used

    def gather(tok_ref):
        def body(r, carry):
            _row_copy(h_hbm, xbuf, sem, tok_ref[0, 0, r], r).start()
            return carry
        lax.fori_loop(0, tile, body, 0, unroll=8)

    @pl.when((i == 0) & (s == 0))
    def _():
        gather(tokc_ref)

    @pl.when(live & (s == 0))
    def _():
        def body(r, carry):
            _row_copy(h_hbm, xbuf, sem, 0, r).wait()
            return carry
        lax.fori_loop(0, tile, body, 0, unroll=8)

    slabs = h_hbm.shape[1] // n_kc
    for c in range(n_kc):
        @pl.when(live & (s == c))
        def _(c=c):
            xk = jnp.concatenate(
                [xbuf[pl.ds(c * slabs + t, tile, stride=SLAB_PITCH), :] for t in range(slabs)],
                axis=1).astype(_BF)
            g = jnp.dot(xk, wg_ref[...], preferred_element_type=_F32)
            u = jnp.dot(xk, wu_ref[...], preferred_element_type=_F32)
            if c > 0:
                g = g + accg[...]
                u = u + accu[...]
            if c < n_kc - 1:
                accg[...] = g
                accu[...] = u
            else:
                act[...] = (g * _sigmoid(g) * u).astype(_BF)

    @pl.when(live & (s == n_kc) & (i + 1 < n_used))
    def _():
        gather(tokn_ref)

    @pl.when(live & (s >= n_kc))
    def _():
        res = _dot_halves(act[...], wd_ref[...])
        for t in range(o_ref.shape[1]):
            o_ref[:, t, :] = res[:, t * V7X_LANES:(t + 1) * V7X_LANES]

    @pl.when(jnp.logical_not(live) & (s >= n_kc))
    def _():
        o_ref[...] = jnp.zeros_like(o_ref)


def _routed_experts(h_rows, tok_of_row, tile_expert, n_used, w_gate, w_up, w_down):
    tile = MOE_TILE
    n_tiles = tile_expert.shape[0]
    d, f = w_gate.shape[1:]
    n_kc, n_nc = d // MOE_KC, d // MOE_NC
    slabs = d // V7X_LANES
    tok3 = tok_of_row.reshape(n_tiles, 1, tile)

    def k_idx(i, s, te, nu):
        return jnp.where(i < nu[0], jnp.minimum(s, n_kc - 1), n_kc - 1)

    def n_idx(s):
        return jnp.clip(s - n_kc, 0, n_nc - 1)

    smem_row = lambda fn: pl.BlockSpec((1, 1, tile), fn, memory_space=pltpu.SMEM)
    in_specs = [
        smem_row(lambda i, s, te, nu: (i, 0, 0)),
        smem_row(lambda i, s, te, nu: (jnp.minimum(i + 1, n_tiles - 1), 0, 0)),
        pl.BlockSpec(memory_space=pl.ANY),
        pl.BlockSpec((None, MOE_KC, f), lambda i, s, te, nu: (te[i], k_idx(i, s, te, nu), 0)),
        pl.BlockSpec((None, MOE_KC, f), lambda i, s, te, nu: (te[i], k_idx(i, s, te, nu), 0)),
        pl.BlockSpec((None, f, MOE_NC), lambda i, s, te, nu: (te[i], 0, n_idx(s))),
    ]
    return pl.pallas_call(
        functools.partial(_moe_kernel, tile=tile, n_kc=n_kc),
        grid_spec=pltpu.PrefetchScalarGridSpec(
            num_scalar_prefetch=2,
            grid=(n_tiles, n_kc + n_nc),
            in_specs=in_specs,
            out_specs=pl.BlockSpec((tile, MOE_NC // V7X_LANES, V7X_LANES),
                                   lambda i, s, te, nu: (i, n_idx(s), 0)),
            scratch_shapes=[pltpu.VMEM((tile * SLAB_PITCH, V7X_LANES), _F32), pltpu.VMEM((tile, f), _F32),
                            pltpu.VMEM((tile, f), _F32), pltpu.VMEM((tile, f), _BF),
                            pltpu.SemaphoreType.DMA(())]),
        out_shape=jax.ShapeDtypeStruct((n_tiles * tile, slabs, V7X_LANES), _F32),
        compiler_params=_cparams(2, 56),
        name="routed_experts",
    )(tile_expert, n_used, tok3, tok3, h_rows, w_gate, w_up, w_down)


def _shared_kernel(x_ref, wg_ref, wu_ref, wd_ref, o_ref, accg, accu, act, *, n_kc):
    s = pl.program_id(1)
    for c in range(n_kc):
        @pl.when(s == c)
        def _(c=c):
            xk = x_ref[...].astype(_BF)
            g = jnp.dot(xk, wg_ref[...], preferred_element_type=_F32)
            u = jnp.dot(xk, wu_ref[...], preferred_element_type=_F32)
            if c > 0:
                g = g + accg[...]
                u = u + accu[...]
            if c < n_kc - 1:
                accg[...] = g
                accu[...] = u
            else:
                act[...] = (g * _sigmoid(g) * u).astype(_BF)

    @pl.when(s >= n_kc)
    def _():
        o_ref[...] = _dot_halves(act[...], wd_ref[...])


def _shared_expert(h, wg_bf, wu_bf, wd_bf, tm):
    m, d = h.shape
    f = wg_bf.shape[1]
    n_kc, n_nc = d // MOE_KC, d // SHARED_NC
    k_idx = lambda s: jnp.minimum(s, n_kc - 1)
    n_idx = lambda s: jnp.clip(s - n_kc, 0, n_nc - 1)
    return pl.pallas_call(
        functools.partial(_shared_kernel, n_kc=n_kc),
        grid=(m // tm, n_kc + n_nc),
        in_specs=[pl.BlockSpec((tm, MOE_KC), lambda i, s: (i, k_idx(s))),
                  pl.BlockSpec((MOE_KC, f), lambda i, s: (k_idx(s), 0)),
                  pl.BlockSpec((MOE_KC, f), lambda i, s: (k_idx(s), 0)),
                  pl.BlockSpec((f, SHARED_NC), lambda i, s: (0, n_idx(s)))],
        out_specs=pl.BlockSpec((tm, SHARED_NC), lambda i, s: (i, n_idx(s))),
        out_shape=jax.ShapeDtypeStruct((m, d), _F32),
        scratch_shapes=[pltpu.VMEM((tm, f), _F32), pltpu.VMEM((tm, f), _F32),
                        pltpu.VMEM((tm, f), _BF)],
        compiler_params=_cparams(2, 40),
        name="shared_expert",
    )(h, wg_bf, wu_bf, wd_bf)


def _combine_kernel(destc_ref, destn_ref, gate_ref, h_ref, sh_ref, outs_hbm, g_ref, b_ref,
                    yp_ref, ys_ref, gbuf, pre_ref, sem, *, alpha, n_prompt_tiles):
    i = pl.program_id(0)
    n_steps = pl.num_programs(0)
    tt = h_ref.shape[0]
    slabs = outs_hbm.shape[1]
    slot = lax.rem(i, 2)

    def region(sl, k):
        base = (sl * TOP_K + k) * (tt * SLAB_PITCH)
        return base if isinstance(base, int) else pl.multiple_of(base, 8)

    def copy(src_row, sl, k, r):
        rows = pl.ds(region(sl, k) + pl.multiple_of(r * SLAB_PITCH, 8), slabs)
        return pltpu.make_async_copy(outs_hbm.at[src_row], gbuf.at[rows], sem.at[sl])

    def issue(dest_ref, sl):
        def body(r, carry):
            for k in range(TOP_K):
                copy(dest_ref[0, 0, r * TOP_K + k], sl, k, r).start()
            return carry
        lax.fori_loop(0, tt, body, 0, unroll=2)

    @pl.when(i == 0)
    def _():
        issue(destc_ref, 0)

    @pl.when(i + 1 < n_steps)
    def _():
        issue(destn_ref, 1 - slot)

    def drain(r, carry):
        for k in range(TOP_K):
            copy(0, slot, k, r).wait()
        return carry

    lax.fori_loop(0, tt, drain, 0, unroll=2)
    gate = gate_ref[...]
    gate_b = [jnp.broadcast_to(gate[:, k:k + 1], (tt, V7X_LANES)) for k in range(TOP_K)]
    for s in range(slabs):
        cols = slice(s * V7X_LANES, (s + 1) * V7X_LANES)
        acc = alpha * h_ref[:, cols] + sh_ref[:, cols]
        for k in range(TOP_K):
            acc = acc + gate_b[k] * gbuf[pl.ds(region(slot, k) + s, tt, stride=SLAB_PITCH), :]
        pre_ref[:, cols] = acc
    y = _layer_norm(pre_ref[...], g_ref[...], b_ref[...])

    @pl.when(i < n_prompt_tiles)
    def _():
        yp_ref[...] = y

    @pl.when(i >= n_prompt_tiles)
    def _():
        ys_ref[...] = y


def _combine(dest, gate_lanes, h_all, shared_all, outs, ln_g, ln_b, alpha, n_prompt, n_sample):
    tt = TOK_TILE
    t_all, d = h_all.shape
    n_pt, n_st = n_prompt // tt, n_sample // tt
    n_steps = n_pt + n_st
    dest3 = dest.reshape(n_steps, 1, tt * TOP_K)
    row = lambda w: pl.BlockSpec((tt, w), lambda i: (i, 0))
    vec = pl.BlockSpec((1, d), lambda i: (0, 0))
    dest_spec = lambda fn: pl.BlockSpec((1, 1, tt * TOP_K), fn, memory_space=pltpu.SMEM)
    return pl.pallas_call(
        functools.partial(_combine_kernel, alpha=alpha, n_prompt_tiles=n_pt),
        grid=(n_steps,),
        in_specs=[dest_spec(lambda i: (i, 0, 0)),
                  dest_spec(lambda i: (jnp.minimum(i + 1, n_steps - 1), 0, 0)),
                  row(V7X_LANES), row(d), row(d), pl.BlockSpec(memory_space=pl.ANY), vec, vec],
        out_specs=(pl.BlockSpec((tt, d), lambda i: (jnp.minimum(i, n_pt - 1), 0)),
                   pl.BlockSpec((tt, d), lambda i: (jnp.clip(i - n_pt, 0, n_st - 1), 0))),
        out_shape=(jax.ShapeDtypeStruct((n_prompt, d), _F32),
                   jax.ShapeDtypeStruct((n_sample, d), _F32)),
        scratch_shapes=[pltpu.VMEM((2 * TOP_K * tt * SLAB_PITCH, V7X_LANES), _F32),
                        pltpu.VMEM((tt, d), _F32), pltpu.SemaphoreType.DMA((2,))],
        compiler_params=_cparams(1, 48),
        name="moe_combine",
    )(dest3, dest3, gate_lanes, h_all, shared_all, outs, ln_g.reshape(1, -1), ln_b.reshape(1, -1))


def _routing_tables(idx_lanes, rank_lanes, counts_row):
    n_tok = idx_lanes.shape[0]
    n_assign = n_tok * TOP_K
    n_tiles = -(-n_assign // MOE_TILE) + N_EXPERTS
    idx = idx_lanes[:, :TOP_K]
    counts = counts_row[0, :N_EXPERTS].astype(jnp.int32)
    padded = (counts + MOE_TILE - 1) // MOE_TILE * MOE_TILE
    pad_end = jnp.cumsum(padded)
    pad_start = pad_end - padded
    dest = pad_start[idx] + rank_lanes[:, :TOP_K]
    n_used = pad_end[-1] // MOE_TILE
    tiles = jnp.minimum(jnp.arange(n_tiles, dtype=jnp.int32), n_used - 1)
    ends_before = (pad_end[None, :] <= (tiles * MOE_TILE)[:, None]).astype(jnp.int32)
    tile_expert = jnp.minimum(jnp.sum(ends_before, axis=1), N_EXPERTS - 1)
    tok_of_row = jnp.zeros((n_tiles * MOE_TILE,), jnp.int32).at[dest.reshape(-1)].set(
        jnp.arange(n_assign, dtype=jnp.int32) // TOP_K, unique_indices=True)
    return (dest.astype(jnp.int32), tok_of_row, tile_expert.astype(jnp.int32),
            n_used.astype(jnp.int32).reshape(1))


def _mixers_to_router(x2, o_a, o_b, sga, sgb, wts, alpha, tm):
    merged = _merge(o_a, o_b, wts["w_proj_a"], wts["w_proj_b"], sga, sgb, tm)
    return _out_router(merged, x2, wts["w_out"], wts["ln1_g"], wts["ln1_b"],
                       wts["router_w"], wts["router_bias"], alpha, min(tm, ROUTER_TILE))


def _layer(yp, ys, cache_k, cache_v, page_table, wts, alpha):
    b, s, d = yp.shape
    bd, t, _ = ys.shape
    n_p, n_s = b * s, bd * t
    xp, xs = yp.reshape(n_p, d), ys.reshape(n_s, d)

    q, k, kb, v, vb, u, sv, sga, sgb = _in_projection(xp, wts["w_in"], ROW_TILE)
    o_a = _sb_prompt(q, kb, vb, wts["sb_bias"], b, s)
    o_b = _sgu_prompt(sv, u, wts["sgu_w"], wts["sgu_b"], wts["sgu_ln_g"], wts["sgu_ln_b"], ROW_TILE)
    h_p, hr_p, idx_p, gate_p = _mixers_to_router(xp, o_a, o_b, sga, sgb, wts, alpha, ROW_TILE)
    k_p = k.reshape(b, s, SB_HEADS, SB_HEAD_DIM)
    v_p = v.reshape(b, s, SB_HEADS, SB_HEAD_DIM)

    q, k, kb, v, vb, u, sv, sga, sgb = _in_projection(xs, wts["w_in"], n_s)
    o_a = _sb_sample(q.astype(_F32), k, v, cache_k, cache_v, page_table, wts["sb_bias"])
    svn, o_b = _sgu_sample(sv, u, wts["sgu_w"], wts["sgu_b"], wts["sgu_ln_g"], wts["sgu_ln_b"])
    h_s, hr_s, idx_s, gate_s = _mixers_to_router(xs, o_a.astype(_BF), o_b, sga, sgb, wts, alpha, n_s)
    k_s = k.reshape(bd, t, SB_HEADS, SB_HEAD_DIM)
    v_s = v.reshape(bd, t, SB_HEADS, SB_HEAD_DIM)

    h_all = jnp.concatenate([h_p, h_s], axis=0)
    h_rows = jnp.concatenate([hr_p, hr_s], axis=0)
    idx_all = jnp.concatenate([idx_p, idx_s], axis=0)
    gate_all = jnp.concatenate([gate_p, gate_s], axis=0)
    rank, counts = _ranks(idx_all)
    dest, tok_of_row, tile_expert, n_used = _routing_tables(idx_all, rank, counts)
    outs = _routed_experts(h_rows, tok_of_row, tile_expert, n_used,
                           wts["exp_w_gate"], wts["exp_w_up"], wts["exp_w_down"])
    sh_p = _shared_expert(h_p, wts["sh_w_gate"], wts["sh_w_up"], wts["sh_w_down"], ROW_TILE)
    sh_s = _shared_expert(h_s, wts["sh_w_gate"], wts["sh_w_up"], wts["sh_w_down"], n_s)
    shared_all = jnp.concatenate([sh_p, sh_s], axis=0)
    yp2, ys2 = _combine(dest, gate_all, h_all, shared_all, outs, wts["ln2_g"], wts["ln2_b"],
                        alpha, n_p, n_s)
    return (yp2.reshape(b, s, d), ys2.reshape(bd, t, d), k_p, v_p, k_s, v_s,
            svn.reshape(bd, t, SGU_WIDTH))


def kernel(x_prompt, x_sample, cache_k, cache_v, page_table, w_in, sb_bias, sgu_ln_g, sgu_ln_b, sgu_w, sgu_b, w_proj_a, w_proj_b, w_out, ln1_g, ln1_b, router_w, router_bias, exp_w_gate, exp_w_up, exp_w_down, sh_w_gate, sh_w_up, sh_w_down, ln2_g, ln2_b):
    depth = w_in.shape[0]
    alpha = (2.0 * depth) ** 0.25
    yp, ys = x_prompt, x_sample
    kp, vp, ksm, vsm, svs = [], [], [], [], []
    for l in range(depth):
        wts = {
            "w_in": w_in[l].astype(_BF), "sb_bias": sb_bias[l],
            "sgu_ln_g": sgu_ln_g[l], "sgu_ln_b": sgu_ln_b[l], "sgu_w": sgu_w[l], "sgu_b": sgu_b[l],
            "w_proj_a": w_proj_a[l].astype(_BF), "w_proj_b": w_proj_b[l].astype(_BF),
            "w_out": w_out[l].astype(_BF), "ln1_g": ln1_g[l], "ln1_b": ln1_b[l],
            "router_w": router_w[l].astype(_BF), "router_bias": router_bias[l],
            "exp_w_gate": exp_w_gate[l].astype(_BF), "exp_w_up": exp_w_up[l].astype(_BF),
            "exp_w_down": exp_w_down[l].astype(_BF),
            "sh_w_gate": sh_w_gate[l].astype(_BF), "sh_w_up": sh_w_up[l].astype(_BF),
            "sh_w_down": sh_w_down[l].astype(_BF), "ln2_g": ln2_g[l], "ln2_b": ln2_b[l],
        }
        yp, ys, k_p, v_p, k_s, v_s, svn = _layer(yp, ys, cache_k[l], cache_v[l], page_table,
                                                 wts, alpha)
        kp.append(k_p)
        vp.append(v_p)
        ksm.append(k_s)
        vsm.append(v_s)
        svs.append(svn)
    return (yp, ys, jnp.stack(kp), jnp.stack(vp), jnp.stack(ksm), jnp.stack(vsm), jnp.stack(svs))
```

```python
import functools

import jax
import jax.numpy as jnp
from jax import lax
from jax.experimental import pallas as pl
from jax.experimental.pallas import tpu as pltpu

D_MODEL = 2048
SB_HEADS = 16
SB_HEAD_DIM = 64
SB_WIDTH = SB_HEADS * SB_HEAD_DIM
SGU_GROUPS = 8
SGU_GROUP_DIM = 128
SGU_WIDTH = SGU_GROUPS * SGU_GROUP_DIM
CHUNK = 128
N_EXPERTS = 64
TOP_K = 6
D_EXPERT = 1408
ROUTED_SCALE = 2.5
LN_EPS = 1e-5
SB_SCALE = SB_HEAD_DIM ** -0.5

V7X_LANES = 128
V7X_VMEM_BYTES = 64 * 2 ** 20

ROW_TILE = 512
COL_TILE = 512
ATT_TQ = 256
ATT_KG = 512
ATT_PAGES = 4
MOE_TILE = 512
MOE_KC = 512
MOE_NC = 1024
SHARED_NC = 512
TOK_TILE = 128
ROUTER_TILE = 256
SLAB_PITCH = 24

_BF = jnp.bfloat16
_F32 = jnp.float32


def _cparams(n_axes, vmem_mib):
    return pltpu.CompilerParams(
        dimension_semantics=("arbitrary",) * n_axes,
        vmem_limit_bytes=min(vmem_mib * 2 ** 20, V7X_VMEM_BYTES - 6 * 2 ** 20))


def _layer_norm(x, g, b):
    mu = jnp.mean(x, axis=-1, keepdims=True)
    xc = x - mu
    var = jnp.mean(xc * xc, axis=-1, keepdims=True)
    return xc * lax.rsqrt(var + LN_EPS) * g + b


def _gelu(x):
    return 0.5 * x * (1.0 + lax.erf(x * 0.7071067811865476))


def _sigmoid(x):
    return 1.0 / (1.0 + jnp.exp(-x))


def _softplus(z):
    return jnp.maximum(z, 0.0) + jnp.log(1.0 + jnp.exp(-jnp.abs(z)))


def _dot_halves(x, w):
    half = w.shape[1] // 2
    return jnp.concatenate([jnp.dot(x, w[:, :half], preferred_element_type=_F32),
                            jnp.dot(x, w[:, half:], preferred_element_type=_F32)], axis=1)


def _split_dot(x, u):
    hi = x.astype(_BF)
    lo = (x - hi.astype(_F32)).astype(_BF)
    return (jnp.dot(hi, u, preferred_element_type=_F32)
            + jnp.dot(lo, u, preferred_element_type=_F32))


def _in_sections(tn):
    widths = (SB_WIDTH, SB_WIDTH, SB_WIDTH, SGU_WIDTH, SGU_WIDTH, D_MODEL, D_MODEL)
    starts, s = [], 0
    for w in widths:
        starts.append(s // tn)
        s += w
    return starts, [w // tn for w in widths]


def _in_proj_kernel(x_ref, w_ref, q_ref, k_ref, kb_ref, v_ref, vb_ref, u_ref, sv_ref,
                    ga_ref, gb_ref, xb_ref, *, starts, counts):
    j = pl.program_id(1)

    @pl.when(j == 0)
    def _():
        xb_ref[...] = x_ref[...].astype(_BF)

    acc = _dot_halves(xb_ref[...], w_ref[...])

    def section(n):
        return (j >= starts[n]) & (j < starts[n] + counts[n])

    @pl.when(section(0))
    def _():
        q_ref[...] = (acc * SB_SCALE).astype(_BF)

    @pl.when(section(1))
    def _():
        k_ref[...] = acc
        kb_ref[...] = acc.astype(_BF)

    @pl.when(section(2))
    def _():
        v_ref[...] = acc
        vb_ref[...] = acc.astype(_BF)

    @pl.when(section(3))
    def _():
        u_ref[...] = _gelu(acc)

    @pl.when(section(4))
    def _():
        sv_ref[...] = _gelu(acc)

    @pl.when(section(5))
    def _():
        ga_ref[...] = _sigmoid(acc)

    @pl.when(section(6))
    def _():
        gb_ref[...] = _sigmoid(acc)


def _in_projection(x, w_bf, tm):
    m, d = x.shape
    tn = COL_TILE
    starts, counts = _in_sections(tn)
    n_tiles = w_bf.shape[1] // tn

    def out_spec(n):
        return pl.BlockSpec(
            (tm, tn), lambda i, j, n=n: (i, jnp.clip(j - starts[n], 0, counts[n] - 1)))

    shp = lambda w, dt: jax.ShapeDtypeStruct((m, w), dt)
    out_shape = (shp(SB_WIDTH, _BF), shp(SB_WIDTH, _F32), shp(SB_WIDTH, _BF),
                 shp(SB_WIDTH, _F32), shp(SB_WIDTH, _BF), shp(SGU_WIDTH, _F32),
                 shp(SGU_WIDTH, _F32), shp(D_MODEL, _F32), shp(D_MODEL, _F32))
    out_specs = (out_spec(0), out_spec(1), out_spec(1), out_spec(2), out_spec(2),
                 out_spec(3), out_spec(4), out_spec(5), out_spec(6))
    return pl.pallas_call(
        functools.partial(_in_proj_kernel, starts=starts, counts=counts),
        grid=(m // tm, n_tiles),
        in_specs=[pl.BlockSpec((tm, d), lambda i, j: (i, 0)),
                  pl.BlockSpec((d, tn), lambda i, j: (0, j))],
        out_specs=out_specs,
        out_shape=out_shape,
        scratch_shapes=[pltpu.VMEM((tm, d), _BF)],
        compiler_params=_cparams(2, 40),
        name="in_projection",
    )(x, w_bf)


def _sb_prompt_kernel(bias_ref, q_ref, k_ref, v_ref, o_ref, *, tq, kg):
    p = pl.program_id(1)
    i = pl.program_id(2)
    nb = kg // V7X_LANES
    q2 = q_ref[...].astype(_F32)
    lane = lax.broadcasted_iota(jnp.int32, (tq, V7X_LANES), 1)
    row_pos = i * tq + lax.broadcasted_iota(jnp.int32, (tq, kg), 0)
    col = lax.broadcasted_iota(jnp.int32, (tq, kg), 1)
    ur = lax.broadcasted_iota(jnp.int32, (2 * V7X_LANES, 2 * V7X_LANES), 0)
    uc = lax.broadcasted_iota(jnp.int32, (2 * V7X_LANES, 2 * V7X_LANES), 1)
    later_in_block = jnp.where(((ur >= V7X_LANES) == (uc >= V7X_LANES)) & (ur > uc),
                               1.0, 0.0).astype(_BF)
    n_groups = ((i + 1) * tq + kg - 1) // kg
    in_head = [(lane >= hh * SB_HEAD_DIM) & (lane < (hh + 1) * SB_HEAD_DIM) for hh in range(2)]
    qm = [jnp.where(in_head[hh], q2, 0.0).astype(_BF) for hh in range(2)]
    bias = [bias_ref[2 * p + hh] for hh in range(2)]

    def body(t, carry):
        off = pl.multiple_of((n_groups - 1 - t) * kg, kg)
        kblk = k_ref[pl.ds(off, kg), :]
        vblk = v_ref[pl.ds(off, kg), :]
        causal = (off + col) < row_pos
        new = []
        for hh in range(2):
            run, acc = carry[2 * hh], carry[2 * hh + 1]
            z = lax.dot_general(qm[hh], kblk, (((1,), (1,)), ((), ())),
                                preferred_element_type=_F32) + bias[hh]
            sp = _softplus(z)
            spm = jnp.where(causal, sp, 0.0)
            spb = spm.astype(_BF)
            pairs = jnp.concatenate(
                [spb[:, g * 2 * V7X_LANES:(g + 1) * 2 * V7X_LANES] for g in range(nb // 2)], axis=0)
            within = jnp.dot(pairs, later_in_block, preferred_element_type=_F32)
            a_blocks = [None] * nb
            for c in reversed(range(nb)):
                cols = slice(c * V7X_LANES, (c + 1) * V7X_LANES)
                w_c = within[(c // 2) * tq:(c // 2 + 1) * tq,
                             (c % 2) * V7X_LANES:(c % 2 + 1) * V7X_LANES]
                log_a = z[:, cols] - sp[:, cols] - w_c - run
                a_blocks[c] = jnp.where(causal[:, cols], jnp.exp(log_a), 0.0).astype(_BF)
                run = run + jnp.sum(spm[:, cols], axis=1, keepdims=True)
            a = jnp.concatenate(a_blocks, axis=1)
            acc = acc + jnp.dot(a, vblk, preferred_element_type=_F32)
            new += [run, acc]
        return tuple(new)

    zero = jnp.zeros((tq, V7X_LANES), _F32)
    zero_col = jnp.zeros((tq, 1), _F32)
    res = lax.fori_loop(0, n_groups, body, (zero_col, zero, zero_col, zero))
    o_ref[...] = jnp.where(in_head[0], res[1], res[3]).astype(_BF)


def _sb_prompt(q_bf, k_bf, v_bf, sb_bias, b, s):
    tq, kg = min(ATT_TQ, s), min(ATT_KG, s)
    q3, k3, v3 = (a.reshape(b, s, SB_WIDTH) for a in (q_bf, k_bf, v_bf))
    kv_spec = pl.BlockSpec((None, s, V7X_LANES), lambda bi, p, i, bias: (bi, 0, p))
    q_spec = pl.BlockSpec((None, tq, V7X_LANES), lambda bi, p, i, bias: (bi, i, p))
    out = pl.pallas_call(
        functools.partial(_sb_prompt_kernel, tq=tq, kg=kg),
        grid_spec=pltpu.PrefetchScalarGridSpec(
            num_scalar_prefetch=1,
            grid=(b, SB_WIDTH // V7X_LANES, s // tq),
            in_specs=[q_spec, kv_spec, kv_spec],
            out_specs=q_spec),
        out_shape=jax.ShapeDtypeStruct((b, s, SB_WIDTH), _BF),
        compiler_params=_cparams(3, 32),
        name="sb_attention_prompt",
    )(sb_bias.astype(_F32), q3, k3, v3)
    return out.reshape(b * s, SB_WIDTH)


def _sb_sample_kernel(pt_ref, q_ref, kn_ref, vn_ref, bias_ref, *refs, page, pps, past_len):
    kc_refs, vc_refs = refs[:pps], refs[pps:2 * pps]
    o_ref, acc_ref, run_ref = refs[2 * pps:]
    j = pl.program_id(1)
    n_steps = pl.num_programs(1)
    head = lax.broadcasted_iota(jnp.int32, (SB_HEADS, SB_WIDTH), 0)
    lane = lax.broadcasted_iota(jnp.int32, (SB_HEADS, SB_WIDTH), 1)
    in_head = (lane >= head * SB_HEAD_DIM) & (lane < (head + 1) * SB_HEAD_DIM)
    q_bd = jnp.where(in_head, q_ref[...], 0.0).astype(_BF)
    bias = bias_ref[...]
    nt = (((1,), (1,)), ((), ()))

    @pl.when(j == 0)
    def _():
        k_new = kn_ref[...].astype(_BF).astype(_F32)
        v_new = vn_ref[...].astype(_BF).astype(_F32)
        z_new = jnp.sum(q_bd.astype(_F32) * k_new, axis=1, keepdims=True) + bias
        k_pos = past_len + lax.broadcasted_iota(jnp.int32, (SB_HEADS, 1), 1)
        q_pos = past_len + lax.broadcasted_iota(jnp.int32, (SB_HEADS, 1), 1)
        causal = k_pos < q_pos
        sp_new = _softplus(z_new)
        a_new = jnp.where(causal, jnp.exp(z_new - sp_new), 0.0).astype(_BF).astype(_F32)
        acc_ref[...] = a_new * v_new
        run_ref[...] = jnp.where(causal, sp_new, 0.0)

    ur = lax.broadcasted_iota(jnp.int32, (page, page), 0)
    uc = lax.broadcasted_iota(jnp.int32, (page, page), 1)
    upper = jnp.where(ur > uc, 1.0, 0.0).astype(_BF)
    zs = [jnp.dot(q_bd, kc[...].astype(_BF), preferred_element_type=_F32) + bias
          for kc in kc_refs]
    run = run_ref[...]
    acc = acc_ref[...]
    for m in reversed(range(pps)):
        sp = _softplus(zs[m])
        within = _split_dot(sp, upper)
        a = jnp.exp(zs[m] - sp - within - run).astype(_BF)
        acc = acc + lax.dot_general(a, vc_refs[m][...].astype(_BF), nt,
                                    preferred_element_type=_F32)
        run = run + jnp.sum(sp, axis=1, keepdims=True)
    acc_ref[...] = acc
    run_ref[...] = run

    @pl.when(j == n_steps - 1)
    def _():
        o_ref[...] = jnp.sum(jnp.where(in_head, acc, 0.0), axis=0, keepdims=True)


def _sb_sample(q, k_new, v_new, cache_k, cache_v, page_table, sb_bias):
    bd = q.shape[0]
    n_pages = page_table.shape[1]
    n_pool, page = cache_k.shape[:2]
    pps = max(p for p in range(1, ATT_PAGES + 1) if n_pages % p == 0)
    ck = jnp.transpose(cache_k, (0, 2, 3, 1)).reshape(n_pool, SB_WIDTH, page)
    cv = jnp.transpose(cache_v, (0, 2, 3, 1)).reshape(n_pool, SB_WIDTH, page)
    row = pl.BlockSpec((None, 1, SB_WIDTH), lambda b, j, pt: (b, 0, 0))

    def page_spec(m):
        return pl.BlockSpec(
            (None, SB_WIDTH, page),
            lambda b, j, pt, m=m: (pt[b * n_pages + n_pages - (j + 1) * pps + m], 0, 0))

    r3 = lambda a: a.reshape(bd, 1, SB_WIDTH)
    out = pl.pallas_call(
        functools.partial(_sb_sample_kernel, page=page, pps=pps, past_len=n_pages * page),
        grid_spec=pltpu.PrefetchScalarGridSpec(
            num_scalar_prefetch=1,
            grid=(bd, n_pages // pps),
            in_specs=([row, row, row, pl.BlockSpec((SB_HEADS, 1), lambda b, j, pt: (0, 0))]
                      + [page_spec(m) for m in range(pps)] * 2),
            out_specs=row,
            scratch_shapes=[pltpu.VMEM((SB_HEADS, SB_WIDTH), _F32),
                            pltpu.VMEM((SB_HEADS, 1), _F32)]),
        out_shape=jax.ShapeDtypeStruct((bd, 1, SB_WIDTH), _F32),
        compiler_params=_cparams(2, 32),
        name="sb_attention_sample",
    )(page_table.reshape(-1).astype(jnp.int32), r3(q), r3(k_new), r3(v_new),
      sb_bias.astype(_F32).reshape(SB_HEADS, 1), *([ck] * pps), *([cv] * pps))
    return out.reshape(bd, SB_WIDTH)


def _sgu_prompt_kernel(sv_ref, u_ref, w_ref, bias_ref, g_ref, b_ref, o_ref, *, n_chunks):
    r = lax.broadcasted_iota(jnp.int32, (CHUNK, CHUNK), 0)
    c = lax.broadcasted_iota(jnp.int32, (CHUNK, CHUNK), 1)
    lower = r >= c
    for ch in range(n_chunks):
        rows = slice(ch * CHUNK, (ch + 1) * CHUNK)
        svn = _layer_norm(sv_ref[rows, :], g_ref[...], b_ref[...]).astype(_BF)
        for g in range(SGU_GROUPS):
            cols = slice(g * SGU_GROUP_DIM, (g + 1) * SGU_GROUP_DIM)
            w = jnp.where(lower, w_ref[g], 0.0).astype(_BF)
            mixed = jnp.dot(w, svn[:, cols], preferred_element_type=_F32) + bias_ref[:, cols]
            o_ref[rows, cols] = (u_ref[rows, cols] * mixed).astype(_BF)


def _sgu_prompt(sv, u, sgu_w, sgu_b, ln_g, ln_b, tm):
    m = sv.shape[0]
    bias_full = jnp.repeat(sgu_b.T, SGU_GROUP_DIM, axis=1)
    blk = pl.BlockSpec((tm, SGU_WIDTH), lambda i: (i, 0))
    full = lambda shape: pl.BlockSpec(shape, lambda i: (0,) * len(shape))
    return pl.pallas_call(
        functools.partial(_sgu_prompt_kernel, n_chunks=tm // CHUNK),
        grid=(m // tm,),
        in_specs=[blk, blk, full((SGU_GROUPS, CHUNK, CHUNK)), full((CHUNK, SGU_WIDTH)),
                  full((1, SGU_WIDTH)), full((1, SGU_WIDTH))],
        out_specs=blk,
        out_shape=jax.ShapeDtypeStruct((m, SGU_WIDTH), _BF),
        compiler_params=_cparams(1, 32),
        name="sgu_prompt",
    )(sv, u, sgu_w, bias_full, ln_g.reshape(1, -1), ln_b.reshape(1, -1))


def _sgu_sample_kernel(sv_ref, u_ref, w0_ref, b0_ref, g_ref, b_ref, svn_ref, o_ref):
    svn = _layer_norm(sv_ref[...], g_ref[...], b_ref[...])
    svn_ref[...] = svn
    w0 = w0_ref[...].astype(_BF).astype(_F32)
    mixed = w0 * svn.astype(_BF).astype(_F32) + b0_ref[...]
    o_ref[...] = (u_ref[...] * mixed).astype(_BF)


def _sgu_sample(sv, u, sgu_w, sgu_b, ln_g, ln_b):
    m = sv.shape[0]
    w0 = jnp.repeat(sgu_w[:, 0, 0], SGU_GROUP_DIM).reshape(1, SGU_WIDTH)
    b0 = jnp.repeat(sgu_b[:, 0], SGU_GROUP_DIM).reshape(1, SGU_WIDTH)
    blk = pl.BlockSpec((m, SGU_WIDTH), lambda i: (0, 0))
    vec = pl.BlockSpec((1, SGU_WIDTH), lambda i: (0, 0))
    return pl.pallas_call(
        _sgu_sample_kernel,
        grid=(1,),
        in_specs=[blk, blk, vec, vec, vec, vec],
        out_specs=(blk, blk),
        out_shape=(jax.ShapeDtypeStruct((m, SGU_WIDTH), _F32),
                   jax.ShapeDtypeStruct((m, SGU_WIDTH), _BF)),
        compiler_params=_cparams(1, 16),
        name="sgu_sample",
    )(sv, u, w0, b0, ln_g.reshape(1, -1), ln_b.reshape(1, -1))


def _merge_kernel(oa_ref, ob_ref, wa_ref, wb_ref, sga_ref, sgb_ref, m_ref):
    pa = jnp.dot(oa_ref[...], wa_ref[...], preferred_element_type=_F32)
    pb = jnp.dot(ob_ref[...], wb_ref[...], preferred_element_type=_F32)
    m_ref[...] = (sga_ref[...] * pa + sgb_ref[...] * pb).astype(_BF)


def _merge(o_a, o_b, wa_bf, wb_bf, sga, sgb, tm):
    m = o_a.shape[0]
    tn = COL_TILE
    act = pl.BlockSpec((tm, SB_WIDTH), lambda i, j: (i, 0))
    wsp = pl.BlockSpec((SB_WIDTH, tn), lambda i, j: (0, j))
    gsp = pl.BlockSpec((tm, tn), lambda i, j: (i, j))
    return pl.pallas_call(
        _merge_kernel,
        grid=(m // tm, D_MODEL // tn),
        in_specs=[act, act, wsp, wsp, gsp, gsp],
        out_specs=gsp,
        out_shape=jax.ShapeDtypeStruct((m, D_MODEL), _BF),
        compiler_params=_cparams(2, 32),
        name="gated_merge",
    )(o_a, o_b, wa_bf, wb_bf, sga, sgb)


def _out_router_kernel(m_ref, x_ref, wo_ref, g_ref, b_ref, rw_ref, rb_ref,
                       h_ref, hrows_ref, idx_ref, gate_ref, *, alpha):
    tm = m_ref.shape[0]
    y = alpha * x_ref[...] + _dot_halves(m_ref[...], wo_ref[...])
    h = _layer_norm(y, g_ref[...], b_ref[...])
    h_ref[...] = h
    for s in range(h.shape[1] // V7X_LANES):
        hrows_ref[:, s, :] = h[:, s * V7X_LANES:(s + 1) * V7X_LANES]
    logits = jnp.dot(h.astype(_BF), rw_ref[...], preferred_element_type=_F32)
    aff = _sigmoid(logits)
    sel = aff + rb_ref[...]
    e_iota = lax.broadcasted_iota(jnp.int32, (tm, N_EXPERTS), 1).astype(_F32)
    lane = lax.broadcasted_iota(jnp.int32, (tm, V7X_LANES), 1)
    idx_out = jnp.zeros((tm, V7X_LANES), _F32)
    gate_out = jnp.zeros((tm, V7X_LANES), _F32)
    gsum = jnp.zeros((tm, 1), _F32)
    for k in range(TOP_K):
        best = jnp.max(sel, axis=1, keepdims=True)
        ik = jnp.min(jnp.where(sel == best, e_iota, float(N_EXPERTS)), axis=1, keepdims=True)
        hit = e_iota == ik
        gk = jnp.sum(jnp.where(hit, aff, 0.0), axis=1, keepdims=True)
        sel = jnp.where(hit, -jnp.inf, sel)
        idx_out = jnp.where(lane == k, ik, idx_out)
        gate_out = jnp.where(lane == k, gk, gate_out)
        gsum = gsum + gk
    idx_ref[...] = idx_out.astype(jnp.int32)
    gate_ref[...] = ROUTED_SCALE * gate_out / gsum


def _out_router(merged, x, wo_bf, ln_g, ln_b, rw_bf, rbias, alpha, tm):
    m = x.shape[0]
    slabs = D_MODEL // V7X_LANES
    row = lambda w: pl.BlockSpec((tm, w), lambda i: (i, 0))
    full = lambda shape: pl.BlockSpec(shape, lambda i: (0, 0))
    return pl.pallas_call(
        functools.partial(_out_router_kernel, alpha=alpha),
        grid=(m // tm,),
        in_specs=[row(D_MODEL), row(D_MODEL), full((D_MODEL, D_MODEL)), full((1, D_MODEL)),
                  full((1, D_MODEL)), full((D_MODEL, N_EXPERTS)), full((1, N_EXPERTS))],
        out_specs=(row(D_MODEL), pl.BlockSpec((tm, slabs, V7X_LANES), lambda i: (i, 0, 0)),
                   row(V7X_LANES), row(V7X_LANES)),
        out_shape=(jax.ShapeDtypeStruct((m, D_MODEL), _F32),
                   jax.ShapeDtypeStruct((m, slabs, V7X_LANES), _F32),
                   jax.ShapeDtypeStruct((m, V7X_LANES), jnp.int32),
                   jax.ShapeDtypeStruct((m, V7X_LANES), _F32)),
        compiler_params=_cparams(1, 48),
        name="out_proj_router",
    )(merged, x, wo_bf, ln_g.reshape(1, -1), ln_b.reshape(1, -1), rw_bf,
      rbias.astype(_F32).reshape(1, -1))


def _rank_kernel(idx_ref, rank_ref, cnt_ref, carry_ref):
    t = pl.program_id(0)
    tt = idx_ref.shape[0]

    @pl.when(t == 0)
    def _():
        carry_ref[...] = jnp.zeros_like(carry_ref)

    idx = idx_ref[...]
    e_iota = lax.broadcasted_iota(jnp.int32, (tt, V7X_LANES), 1)
    hits = [e_iota == idx[:, k:k + 1] for k in range(TOP_K)]
    member = jnp.zeros((tt, V7X_LANES), _F32)
    for hit in hits:
        member = member + jnp.where(hit, 1.0, 0.0)
    r = lax.broadcasted_iota(jnp.int32, (tt, tt), 0)
    c = lax.broadcasted_iota(jnp.int32, (tt, tt), 1)
    earlier = jnp.where(c < r, 1.0, 0.0).astype(_BF)
    before = jnp.dot(earlier, member.astype(_BF), preferred_element_type=_F32) + carry_ref[0:1, :]
    rank = jnp.zeros((tt, V7X_LANES), _F32)
    for k, hit in enumerate(hits):
        rk = jnp.sum(jnp.where(hit, before, 0.0), axis=1, keepdims=True)
        rank = jnp.where(e_iota == k, rk, rank)
    rank_ref[...] = rank.astype(jnp.int32)
    carry_ref[0:1, :] = carry_ref[0:1, :] + jnp.sum(member, axis=0, keepdims=True)
    cnt_ref[...] = carry_ref[...]


def _ranks(idx_lanes):
    t = idx_lanes.shape[0]
    tt = TOK_TILE
    blk = pl.BlockSpec((tt, V7X_LANES), lambda i: (i, 0))
    return pl.pallas_call(
        _rank_kernel,
        grid=(t // tt,),
        in_specs=[blk],
        out_specs=(blk, pl.BlockSpec((8, V7X_LANES), lambda i: (0, 0))),
        out_shape=(jax.ShapeDtypeStruct((t, V7X_LANES), jnp.int32),
                   jax.ShapeDtypeStruct((8, V7X_LANES), _F32)),
        scratch_shapes=[pltpu.VMEM((8, V7X_LANES), _F32)],
        compiler_params=_cparams(1, 16),
        name="expert_ranks",
    )(idx_lanes)


def _slab_rows(r, n_slabs):
    return pl.ds(pl.multiple_of(r * SLAB_PITCH, 8), n_slabs)


def _row_copy(h_hbm, xbuf, sem, tok, r):
    return pltpu.make_async_copy(h_hbm.at[tok], xbuf.at[_slab_rows(r, h_hbm.shape[1])], sem)


def _moe_kernel(te_ref, nu_ref, tokc_ref, tokn_ref, h_hbm, wg_ref, wu_ref, wd_ref, o_ref,
                xbuf, accg, accu, act, sem, *, tile, n_kc):
    i = pl.program_id(0)
    s = pl.program_id(1)
    n_used = nu_ref[0]
    live = i < n_used

    def gather(tok_ref):
        def body(r2, carry):
            for lane in range(2):
                r = 2 * r2 + lane
                _row_copy(h_hbm, xbuf, sem, tok_ref[0, 0, r], r).start(priority=lane)
            return carry
        lax.fori_loop(0, tile // 2, body, 0, unroll=4)

    @pl.when((i == 0) & (s == 0))
    def _():
        gather(tokc_ref)

    @pl.when(live & (s == 0))
    def _():
        def body(r, carry):
            _row_copy(h_hbm, xbuf, sem, 0, r).wait()
            return carry
        lax.fori_loop(0, tile, body, 0, unroll=8)

    slabs = h_hbm.shape[1] // n_kc
    for c in range(n_kc):
        @pl.when(live & (s == c))
        def _(c=c):
            xk = jnp.concatenate(
                [xbuf[pl.ds(c * slabs + t, tile, stride=SLAB_PITCH), :] for t in range(slabs)],
                axis=1).astype(_BF)
            g = jnp.dot(xk, wg_ref[...].astype(_BF), preferred_element_type=_F32)
            u = jnp.dot(xk, wu_ref[...].astype(_BF), preferred_element_type=_F32)
            if c > 0:
                g = g + accg[...]
                u = u + accu[...]
            if c < n_kc - 1:
                accg[...] = g
                accu[...] = u
            else:
                act[...] = (g * _sigmoid(g) * u).astype(_BF)

    @pl.when(live & (s == n_kc) & (i + 1 < n_used))
    def _():
        gather(tokn_ref)

    @pl.when(live & (s >= n_kc))
    def _():
        res = _dot_halves(act[...], wd_ref[...].astype(_BF))
        for t in range(o_ref.shape[1]):
            o_ref[:, t, :] = res[:, t * V7X_LANES:(t + 1) * V7X_LANES]

    @pl.when(jnp.logical_not(live) & (s >= n_kc))
    def _():
        o_ref[...] = jnp.zeros_like(o_ref)


def _routed_experts(h_rows, tok_of_row, tile_expert, n_used, w_gate, w_up, w_down):
    tile = MOE_TILE
    n_tiles = tile_expert.shape[0]
    d, f = w_gate.shape[1:]
    n_kc, n_nc = d // MOE_KC, d // MOE_NC
    slabs = d // V7X_LANES
    tok3 = tok_of_row.reshape(n_tiles, 1, tile)

    def k_idx(i, s, te, nu):
        return jnp.where(i < nu[0], jnp.minimum(s, n_kc - 1), n_kc - 1)

    def n_idx(s):
        return jnp.clip(s - n_kc, 0, n_nc - 1)

    smem_row = lambda fn: pl.BlockSpec((1, 1, tile), fn, memory_space=pltpu.SMEM)
    in_specs = [
        smem_row(lambda i, s, te, nu: (i, 0, 0)),
        smem_row(lambda i, s, te, nu: (jnp.minimum(i + 1, n_tiles - 1), 0, 0)),
        pl.BlockSpec(memory_space=pl.ANY),
        pl.BlockSpec((None, MOE_KC, f), lambda i, s, te, nu: (te[i], k_idx(i, s, te, nu), 0)),
        pl.BlockSpec((None, MOE_KC, f), lambda i, s, te, nu: (te[i], k_idx(i, s, te, nu), 0)),
        pl.BlockSpec((None, f, MOE_NC), lambda i, s, te, nu: (te[i], 0, n_idx(s))),
    ]
    return pl.pallas_call(
        functools.partial(_moe_kernel, tile=tile, n_kc=n_kc),
        grid_spec=pltpu.PrefetchScalarGridSpec(
            num_scalar_prefetch=2,
            grid=(n_tiles, n_kc + n_nc),
            in_specs=in_specs,
            out_specs=pl.BlockSpec((tile, MOE_NC // V7X_LANES, V7X_LANES),
                                   lambda i, s, te, nu: (i, n_idx(s), 0)),
            scratch_shapes=[pltpu.VMEM((tile * SLAB_PITCH, V7X_LANES), _F32), pltpu.VMEM((tile, f), _F32),
                            pltpu.VMEM((tile, f), _F32), pltpu.VMEM((tile, f), _BF),
                            pltpu.SemaphoreType.DMA(())]),
        out_shape=jax.ShapeDtypeStruct((n_tiles * tile, slabs, V7X_LANES), _F32),
        compiler_params=_cparams(2, 56),
        name="routed_experts",
    )(tile_expert, n_used, tok3, tok3, h_rows, w_gate, w_up, w_down)


def _shared_kernel(x_ref, wg_ref, wu_ref, wd_ref, o_ref, accg, accu, act, *, n_kc):
    s = pl.program_id(1)
    for c in range(n_kc):
        @pl.when(s == c)
        def _(c=c):
            xk = x_ref[...].astype(_BF)
            g = jnp.dot(xk, wg_ref[...], preferred_element_type=_F32)
            u = jnp.dot(xk, wu_ref[...], preferred_element_type=_F32)
            if c > 0:
                g = g + accg[...]
                u = u + accu[...]
            if c < n_kc - 1:
                accg[...] = g
                accu[...] = u
            else:
                act[...] = (g * _sigmoid(g) * u).astype(_BF)

    @pl.when(s >= n_kc)
    def _():
        o_ref[...] = _dot_halves(act[...], wd_ref[...])


def _shared_expert(h, wg_bf, wu_bf, wd_bf, tm):
    m, d = h.shape
    f = wg_bf.shape[1]
    n_kc, n_nc = d // MOE_KC, d // SHARED_NC
    k_idx = lambda s: jnp.minimum(s, n_kc - 1)
    n_idx = lambda s: jnp.clip(s - n_kc, 0, n_nc - 1)
    return pl.pallas_call(
        functools.partial(_shared_kernel, n_kc=n_kc),
        grid=(m // tm, n_kc + n_nc),
        in_specs=[pl.BlockSpec((tm, MOE_KC), lambda i, s: (i, k_idx(s))),
                  pl.BlockSpec((MOE_KC, f), lambda i, s: (k_idx(s), 0)),
                  pl.BlockSpec((MOE_KC, f), lambda i, s: (k_idx(s), 0)),
                  pl.BlockSpec((f, SHARED_NC), lambda i, s: (0, n_idx(s)))],
        out_specs=pl.BlockSpec((tm, SHARED_NC), lambda i, s: (i, n_idx(s))),
        out_shape=jax.ShapeDtypeStruct((m, d), _F32),
        scratch_shapes=[pltpu.VMEM((tm, f), _F32), pltpu.VMEM((tm, f), _F32),
                        pltpu.VMEM((tm, f), _BF)],
        compiler_params=_cparams(2, 40),
        name="shared_expert",
    )(h, wg_bf, wu_bf, wd_bf)


def _combine_kernel(destc_ref, destn_ref, gate_ref, h_ref, sh_ref, outs_hbm, g_ref, b_ref,
                    yp_ref, ys_ref, gbuf, pre_ref, sem, *, alpha, n_prompt_tiles):
    i = pl.program_id(0)
    n_steps = pl.num_programs(0)
    tt = h_ref.shape[0]
    slabs = outs_hbm.shape[1]
    slot = lax.rem(i, 2)

    def region(sl, k):
        base = (sl * TOP_K + k) * (tt * SLAB_PITCH)
        return base if isinstance(base, int) else pl.multiple_of(base, 8)

    def copy(src_row, sl, k, r):
        rows = pl.ds(region(sl, k) + pl.multiple_of(r * SLAB_PITCH, 8), slabs)
        return pltpu.make_async_copy(outs_hbm.at[src_row], gbuf.at[rows], sem.at[sl])

    def issue(dest_ref, sl):
        def body(r, carry):
            for k in range(TOP_K):
                copy(dest_ref[0, 0, r * TOP_K + k], sl, k, r).start(priority=k % 2)
            return carry
        lax.fori_loop(0, tt, body, 0, unroll=2)

    @pl.when(i == 0)
    def _():
        issue(destc_ref, 0)

    @pl.when(i + 1 < n_steps)
    def _():
        issue(destn_ref, 1 - slot)

    def drain(r, carry):
        for k in range(TOP_K):
            copy(0, slot, k, r).wait()
        return carry

    lax.fori_loop(0, tt, drain, 0, unroll=2)
    gate = gate_ref[...]
    gate_b = [jnp.broadcast_to(gate[:, k:k + 1], (tt, V7X_LANES)) for k in range(TOP_K)]
    for s in range(slabs):
        cols = slice(s * V7X_LANES, (s + 1) * V7X_LANES)
        acc = alpha * h_ref[:, cols] + sh_ref[:, cols]
        for k in range(TOP_K):
            acc = acc + gate_b[k] * gbuf[pl.ds(region(slot, k) + s, tt, stride=SLAB_PITCH), :]
        pre_ref[:, cols] = acc
    y = _layer_norm(pre_ref[...], g_ref[...], b_ref[...])

    @pl.when(i < n_prompt_tiles)
    def _():
        yp_ref[...] = y

    @pl.when(i >= n_prompt_tiles)
    def _():
        ys_ref[...] = y


def _combine(dest, gate_lanes, h_all, shared_all, outs, ln_g, ln_b, alpha, n_prompt, n_sample):
    tt = TOK_TILE
    t_all, d = h_all.shape
    n_pt, n_st = n_prompt // tt, n_sample // tt
    n_steps = n_pt + n_st
    dest3 = dest.reshape(n_steps, 1, tt * TOP_K)
    row = lambda w: pl.BlockSpec((tt, w), lambda i: (i, 0))
    vec = pl.BlockSpec((1, d), lambda i: (0, 0))
    dest_spec = lambda fn: pl.BlockSpec((1, 1, tt * TOP_K), fn, memory_space=pltpu.SMEM)
    return pl.pallas_call(
        functools.partial(_combine_kernel, alpha=alpha, n_prompt_tiles=n_pt),
        grid=(n_steps,),
        in_specs=[dest_spec(lambda i: (i, 0, 0)),
                  dest_spec(lambda i: (jnp.minimum(i + 1, n_steps - 1), 0, 0)),
                  row(V7X_LANES), row(d), row(d), pl.BlockSpec(memory_space=pl.ANY), vec, vec],
        out_specs=(pl.BlockSpec((tt, d), lambda i: (jnp.minimum(i, n_pt - 1), 0)),
                   pl.BlockSpec((tt, d), lambda i: (jnp.clip(i - n_pt, 0, n_st - 1), 0))),
        out_shape=(jax.ShapeDtypeStruct((n_prompt, d), _F32),
                   jax.ShapeDtypeStruct((n_sample, d), _F32)),
        scratch_shapes=[pltpu.VMEM((2 * TOP_K * tt * SLAB_PITCH, V7X_LANES), _F32),
                        pltpu.VMEM((tt, d), _F32), pltpu.SemaphoreType.DMA((2,))],
        compiler_params=_cparams(1, 48),
        name="moe_combine",
    )(dest3, dest3, gate_lanes, h_all, shared_all, outs, ln_g.reshape(1, -1), ln_b.reshape(1, -1))


def _routing_tables(idx_lanes, rank_lanes, counts_row):
    n_tok = idx_lanes.shape[0]
    n_assign = n_tok * TOP_K
    n_tiles = -(-n_assign // MOE_TILE) + N_EXPERTS
    idx = idx_lanes[:, :TOP_K]
    counts = counts_row[0, :N_EXPERTS].astype(jnp.int32)
    padded = (counts + MOE_TILE - 1) // MOE_TILE * MOE_TILE
    pad_end = jnp.cumsum(padded)
    pad_start = pad_end - padded
    dest = pad_start[idx] + rank_lanes[:, :TOP_K]
    n_used = pad_end[-1] // MOE_TILE
    tiles = jnp.minimum(jnp.arange(n_tiles, dtype=jnp.int32), n_used - 1)
    ends_before = (pad_end[None, :] <= (tiles * MOE_TILE)[:, None]).astype(jnp.int32)
    tile_expert = jnp.minimum(jnp.sum(ends_before, axis=1), N_EXPERTS - 1)
    tok_of_row = jnp.zeros((n_tiles * MOE_TILE,), jnp.int32).at[dest.reshape(-1)].set(
        jnp.arange(n_assign, dtype=jnp.int32) // TOP_K, unique_indices=True)
    return (dest.astype(jnp.int32), tok_of_row, tile_expert.astype(jnp.int32),
            n_used.astype(jnp.int32).reshape(1))


def _mixers_to_router(x2, o_a, o_b, sga, sgb, wts, alpha, tm):
    merged = _merge(o_a, o_b, wts["w_proj_a"], wts["w_proj_b"], sga, sgb, tm)
    return _out_router(merged, x2, wts["w_out"], wts["ln1_g"], wts["ln1_b"],
                       wts["router_w"], wts["router_bias"], alpha, min(tm, ROUTER_TILE))


def _layer(yp, ys, cache_k, cache_v, page_table, wts, alpha):
    b, s, d = yp.shape
    bd, t, _ = ys.shape
    n_p, n_s = b * s, bd * t
    xp, xs = yp.reshape(n_p, d), ys.reshape(n_s, d)

    q, k, kb, v, vb, u, sv, sga, sgb = _in_projection(xp, wts["w_in"], ROW_TILE)
    o_a = _sb_prompt(q, kb, vb, wts["sb_bias"], b, s)
    o_b = _sgu_prompt(sv, u, wts["sgu_w"], wts["sgu_b"], wts["sgu_ln_g"], wts["sgu_ln_b"], ROW_TILE)
    h_p, hr_p, idx_p, gate_p = _mixers_to_router(xp, o_a, o_b, sga, sgb, wts, alpha, ROW_TILE)
    k_p = k.reshape(b, s, SB_HEADS, SB_HEAD_DIM)
    v_p = v.reshape(b, s, SB_HEADS, SB_HEAD_DIM)

    q, k, kb, v, vb, u, sv, sga, sgb = _in_projection(xs, wts["w_in"], n_s)
    o_a = _sb_sample(q.astype(_F32), k, v, cache_k, cache_v, page_table, wts["sb_bias"])
    svn, o_b = _sgu_sample(sv, u, wts["sgu_w"], wts["sgu_b"], wts["sgu_ln_g"], wts["sgu_ln_b"])
    h_s, hr_s, idx_s, gate_s = _mixers_to_router(xs, o_a.astype(_BF), o_b, sga, sgb, wts, alpha, n_s)
    k_s = k.reshape(bd, t, SB_HEADS, SB_HEAD_DIM)
    v_s = v.reshape(bd, t, SB_HEADS, SB_HEAD_DIM)

    h_all = jnp.concatenate([h_p, h_s], axis=0)
    h_rows = jnp.concatenate([hr_p, hr_s], axis=0)
    idx_all = jnp.concatenate([idx_p, idx_s], axis=0)
    gate_all = jnp.concatenate([gate_p, gate_s], axis=0)
    rank, counts = _ranks(idx_all)
    dest, tok_of_row, tile_expert, n_used = _routing_tables(idx_all, rank, counts)
    outs = _routed_experts(h_rows, tok_of_row, tile_expert, n_used,
                           wts["exp_w_gate"], wts["exp_w_up"], wts["exp_w_down"])
    sh_p = _shared_expert(h_p, wts["sh_w_gate"], wts["sh_w_up"], wts["sh_w_down"], ROW_TILE)
    sh_s = _shared_expert(h_s, wts["sh_w_gate"], wts["sh_w_up"], wts["sh_w_down"], n_s)
    shared_all = jnp.concatenate([sh_p, sh_s], axis=0)
    yp2, ys2 = _combine(dest, gate_all, h_all, shared_all, outs, wts["ln2_g"], wts["ln2_b"],
                        alpha, n_p, n_s)
    return (yp2.reshape(b, s, d), ys2.reshape(bd, t, d), k_p, v_p, k_s, v_s,
            svn.reshape(bd, t, SGU_WIDTH))


def kernel(x_prompt, x_sample, cache_k, cache_v, page_table, w_in, sb_bias, sgu_ln_g, sgu_ln_b, sgu_w, sgu_b, w_proj_a, w_proj_b, w_out, ln1_g, ln1_b, router_w, router_bias, exp_w_gate, exp_w_up, exp_w_down, sh_w_gate, sh_w_up, sh_w_down, ln2_g, ln2_b):
    depth = w_in.shape[0]
    alpha = (2.0 * depth) ** 0.25
    yp, ys = x_prompt, x_sample
    kp, vp, ksm, vsm, svs = [], [], [], [], []
    for l in range(depth):
        wts = {
            "w_in": w_in[l].astype(_BF), "sb_bias": sb_bias[l],
            "sgu_ln_g": sgu_ln_g[l], "sgu_ln_b": sgu_ln_b[l], "sgu_w": sgu_w[l], "sgu_b": sgu_b[l],
            "w_proj_a": w_proj_a[l].astype(_BF), "w_proj_b": w_proj_b[l].astype(_BF),
            "w_out": w_out[l].astype(_BF), "ln1_g": ln1_g[l], "ln1_b": ln1_b[l],
            "router_w": router_w[l].astype(_BF), "router_bias": router_bias[l],
            "exp_w_gate": exp_w_gate[l], "exp_w_up": exp_w_up[l], "exp_w_down": exp_w_down[l],
            "sh_w_gate": sh_w_gate[l].astype(_BF), "sh_w_up": sh_w_up[l].astype(_BF),
            "sh_w_down": sh_w_down[l].astype(_BF), "ln2_g": ln2_g[l], "ln2_b": ln2_b[l],
        }
        yp, ys, k_p, v_p, k_s, v_s, svn = _layer(yp, ys, cache_k[l], cache_v[l], page_table,
                                                 wts, alpha)
        kp.append(k_p)
        vp.append(v_p)
        ksm.append(k_s)
        vsm.append(v_s)
        svs.append(svn)
    return (yp, ys, jnp.stack(kp), jnp.stack(vp), jnp.stack(ksm), jnp.stack(vsm), jnp.stack(svs))
```
